```python
import jax, jax.numpy as jnp
from jax import lax
import numpy as np

D_MODEL = 2048
BATCH = 4
SEQ = 2048
DEPTH = 2
DEC_BATCH = 32
DEC_SEQ = 1
PAST_LEN = 8192
PAGE_SIZE = 128

N_A_LAYERS = DEPTH // 2
N_B_LAYERS = DEPTH - N_A_LAYERS
MLSTM_HEADS = 8
MLSTM_DQK = D_MODEL // 16
MLSTM_DV = D_MODEL // MLSTM_HEADS
MLSTM_CHUNK = 64
MLSTM_IN = 2 * MLSTM_HEADS * MLSTM_DQK + 2 * MLSTM_HEADS * MLSTM_DV + 2 * MLSTM_HEADS
SB_HEADS = 16
SB_HEAD_DIM = D_MODEL // SB_HEADS
SB_WIDTH = SB_HEADS * SB_HEAD_DIM
SB_BLOCK = 128
SB_BIAS_INIT = -6.0
D_FF = ((8 * D_MODEL // 3 + 255) // 256) * 256
CONV_W = 3
EPS = 1e-6

kernel_name = 'yoco_mlstm_stickbreaking_convffn_step'


def rmsnorm(x, g):
    xf = x.astype(jnp.float32)
    y = xf * lax.rsqrt(jnp.mean(xf * xf, axis=-1, keepdims=True) + EPS)
    return (y * g.astype(jnp.float32)).astype(x.dtype)


def mlstm_chunkwise(q, k, v, ig, lf, c0, n0, m0):
    bsz, s, h, _ = q.shape
    L = MLSTM_CHUNK if s % MLSTM_CHUNK == 0 else s
    nc = s // L

    def to_chunks(a):
        return jnp.moveaxis(a.reshape((bsz, nc, L) + a.shape[2:]), 1, 0)

    tri = jnp.tril(jnp.ones((L, L), dtype=bool))[None, :, :, None]

    def step(carry, xs):
        c, n, m = carry
        qc, kc, vc, igc, lfc = xs
        b = jnp.cumsum(lfc, axis=1)
        d = b[:, :, None, :] - b[:, None, :, :] + igc[:, None, :, :]
        d = jnp.where(tri, d, -jnp.inf)
        inter = b + m[:, None, :]
        m_row = jnp.maximum(inter, jnp.max(d, axis=2))
        w = jnp.exp(d - m_row[:, :, None, :]) * jnp.einsum('bthk,bshk->btsh', qc, kc)
        w_inter = jnp.exp(inter - m_row)
        num = jnp.einsum('btsh,bshv->bthv', w, vc) + w_inter[..., None] * jnp.einsum('bthk,bhkv->bthv', qc, c)
        den = jnp.sum(w, axis=2) + w_inter * jnp.einsum('bthk,bhk->bth', qc, n)
        den = jnp.maximum(jnp.abs(den), jnp.exp(-m_row))
        hc = num / den[..., None]
        g = b[:, -1, :]
        log_ws = g[:, None, :] - b + igc
        m_new = jnp.maximum(g + m, jnp.max(log_ws, axis=1))
        ws = jnp.exp(log_ws - m_new[:, None, :])
        decay = jnp.exp(g + m - m_new)
        c_new = decay[..., None, None] * c + jnp.einsum('bsh,bshk,bshv->bhkv', ws, kc, vc)
        n_new = decay[..., None] * n + jnp.einsum('bsh,bshk->bhk', ws, kc)
        return (c_new, n_new, m_new), hc

    xs = (to_chunks(q), to_chunks(k), to_chunks(v), to_chunks(ig), to_chunks(lf))
    (c, n, m), hs = lax.scan(step, (c0, n0, m0), xs)
    return jnp.moveaxis(hs, 0, 1).reshape(bsz, s, h, -1), c, n, m


def mlstm_layer(x, c0, n0, m0, g, w_in, b_i, b_f, headnorm, w_out):
    bsz, s, _ = x.shape
    H, DK, DV = MLSTM_HEADS, MLSTM_DQK, MLSTM_DV
    f32 = jnp.float32
    p = rmsnorm(x, g) @ w_in
    cuts = [H * DK, 2 * H * DK, 2 * H * DK + H * DV, 2 * H * DK + 2 * H * DV, 2 * H * DK + 2 * H * DV + H]
    q, k, v, o, i_pre, f_pre = jnp.split(p, cuts, axis=-1)
    q = q.reshape(bsz, s, H, DK).astype(f32)
    k = k.reshape(bsz, s, H, DK).astype(f32) * (DK ** -0.5)
    v = v.reshape(bsz, s, H, DV).astype(f32)
    ig = (i_pre + b_i).astype(f32)
    lf = jax.nn.log_sigmoid((f_pre + b_f).astype(f32))
    hh, c, n, m = mlstm_chunkwise(q, k, v, ig, lf, c0.astype(f32), n0.astype(f32), m0.astype(f32))
    hh = hh * lax.rsqrt(jnp.mean(hh * hh, axis=-1, keepdims=True) + EPS) * headnorm.astype(f32)
    hh = hh.reshape(bsz, s, H * DV).astype(x.dtype) * jax.nn.sigmoid(o)
    return x + hh @ w_out, c.astype(c0.dtype), n.astype(n0.dtype), m.astype(m0.dtype)


def conv_ffn(x, buf, g, w_gate, w_up, conv_w, conv_b, w_down):
    s = x.shape[1]
    h = rmsnorm(x, g)
    u = h @ w_gate
    full = jnp.concatenate([buf.astype(u.dtype), u], axis=1)
    conv = conv_b
    for j in range(CONV_W):
        conv = conv + conv_w[j] * full[:, j:j + s]
    y = (jax.nn.silu(conv) * (h @ w_up)) @ w_down
    return x + y, full[:, s:]


def shared_kv(x, g, w_kv, k_norm):
    bsz, s, _ = x.shape
    k, v = jnp.split(rmsnorm(x, g) @ w_kv, 2, axis=-1)
    k = rmsnorm(k.reshape(bsz, s, SB_HEADS, SB_HEAD_DIM), k_norm)
    return k, v.reshape(bsz, s, SB_HEADS, SB_HEAD_DIM)


def stick_breaking_block(q, k, v, bias, q_pos, k_pos):
    z = jnp.einsum('bqhd,bkhd->bhqk', q, k).astype(jnp.float32) * (SB_HEAD_DIM ** -0.5)
    z = z + bias.astype(jnp.float32)[None, :, None, None]
    valid = (k_pos[None, :] < q_pos[:, None])[None, None]
    neg_log_keep = jnp.where(valid, jax.nn.softplus(z), 0.0)
    rc = lax.cumsum(neg_log_keep, axis=3, reverse=True)
    after = jnp.concatenate([rc[..., 1:], jnp.zeros_like(rc[..., :1])], axis=-1)
    a = jnp.where(valid, jnp.exp(jax.nn.log_sigmoid(z) - after), 0.0)
    return jnp.einsum('bhqk,bkhd->bqhd', a.astype(v.dtype), v)


def stick_breaking(q, k, v, bias, q_pos, k_pos):
    bsz, nq, h, d = q.shape
    if nq % SB_BLOCK != 0:
        return stick_breaking_block(q, k, v, bias, q_pos, k_pos)
    nb = nq // SB_BLOCK
    qb = jnp.moveaxis(q.reshape(bsz, nb, SB_BLOCK, h, d), 1, 0)
    pb = q_pos.reshape(nb, SB_BLOCK)
    ob = lax.map(lambda xs: stick_breaking_block(xs[0], k, v, bias, xs[1], k_pos), (qb, pb))
    return jnp.moveaxis(ob, 0, 1).reshape(bsz, nq, h, d)


def sb_layer(x, k_all, v_all, q_pos, k_pos, g, w_q, q_norm, bias, w_out):
    bsz, s, _ = x.shape
    q = rmsnorm((rmsnorm(x, g) @ w_q).reshape(bsz, s, SB_HEADS, SB_HEAD_DIM), q_norm)
    o = stick_breaking(q, k_all, v_all, bias, q_pos, k_pos)
    return x + o.reshape(bsz, s, SB_WIDTH) @ w_out


def trunk(x, past_k, past_v, c0, n0, m0, conv0, p):
    bsz, s, _ = x.shape
    past = past_k.shape[1]
    q_pos = past + jnp.arange(s, dtype=jnp.int32)
    k_pos = jnp.arange(past + s, dtype=jnp.int32)
    cs, ns, ms, convs = [], [], [], []
    new_k = new_v = k_all = v_all = None
    for layer in range(DEPTH):
        if layer < N_A_LAYERS:
            x, c, n, m = mlstm_layer(x, c0[layer], n0[layer], m0[layer], p['norm_mix_a'][layer], p['w_in_a'][layer],
                                     p['b_igate'][layer], p['b_fgate'][layer], p['headnorm_a'][layer], p['w_out_a'][layer])
            cs.append(c)
            ns.append(n)
            ms.append(m)
        else:
            j = layer - N_A_LAYERS
            x = sb_layer(x, k_all, v_all, q_pos, k_pos, p['norm_mix_b'][j], p['w_q_b'][j], p['q_norm_b'][j],
                         p['sb_bias'][j], p['w_out_b'][j])
        x, buf = conv_ffn(x, conv0[layer], p['norm_ffn'][layer], p['w_gate'][layer], p['w_up'][layer],
                          p['conv_w'][layer], p['conv_b'][layer], p['w_down'][layer])
        convs.append(buf)
        if layer == N_A_LAYERS - 1:
            new_k, new_v = shared_kv(x, p['norm_kv'], p['w_kv'], p['k_norm'])
            k_all = jnp.concatenate([past_k.astype(new_k.dtype), new_k], axis=1)
            v_all = jnp.concatenate([past_v.astype(new_v.dtype), new_v], axis=1)
    return x, new_k, new_v, jnp.stack(cs), jnp.stack(ns), jnp.stack(ms), jnp.stack(convs)


def setup_inputs(seed: int = 0) -> dict:
    key = jax.random.key(seed)
    ks = jax.random.split(key, 32)
    f32 = jnp.float32
    H, DK, DV = MLSTM_HEADS, MLSTM_DQK, MLSTM_DV

    def nrm(i, shape, scale):
        return jax.random.normal(ks[i], shape, f32) * scale

    def gain(i, shape):
        return 1.0 + nrm(i, shape, 0.02)

    n_pages = PAST_LEN // PAGE_SIZE
    used = DEC_BATCH * n_pages
    n_pool = used + max(1, used // 4)
    page_table = jax.random.permutation(ks[0], n_pool)[:used].reshape(DEC_BATCH, n_pages).astype(jnp.int32)
    return {
        'x_prompt': nrm(1, (BATCH, SEQ, D_MODEL), 1.0),
        'x_sample': nrm(2, (DEC_BATCH, DEC_SEQ, D_MODEL), 1.0),
        'cache_k': nrm(3, (n_pool, PAGE_SIZE, SB_HEADS, SB_HEAD_DIM), 1.0),
        'cache_v': nrm(4, (n_pool, PAGE_SIZE, SB_HEADS, SB_HEAD_DIM), 1.0),
        'page_table': page_table,
        'state_mlstm_C': nrm(5, (N_A_LAYERS, DEC_BATCH, H, DK, DV), 0.1),
        'state_mlstm_n': nrm(6, (N_A_LAYERS, DEC_BATCH, H, DK), 0.1),
        'state_mlstm_m': nrm(7, (N_A_LAYERS, DEC_BATCH, H), 0.5),
        'state_conv': nrm(8, (DEPTH, DEC_BATCH, CONV_W - 1, D_FF), 1.0),
        'norm_mix_a': gain(9, (N_A_LAYERS, D_MODEL)),
        'w_in_a': nrm(10, (N_A_LAYERS, D_MODEL, MLSTM_IN), D_MODEL ** -0.5),
        'b_igate': -3.0 + nrm(11, (N_A_LAYERS, H), 0.1),
        'b_fgate': 3.0 + nrm(12, (N_A_LAYERS, H), 0.5),
        'headnorm_a': gain(13, (N_A_LAYERS, H, DV)),
        'w_out_a': nrm(14, (N_A_LAYERS, H * DV, D_MODEL), (H * DV) ** -0.5),
        'norm_kv': gain(15, (D_MODEL,)),
        'w_kv': nrm(16, (D_MODEL, 2 * SB_WIDTH), D_MODEL ** -0.5),
        'k_norm': gain(17, (SB_HEAD_DIM,)),
        'norm_mix_b': gain(18, (N_B_LAYERS, D_MODEL)),
        'w_q_b': nrm(19, (N_B_LAYERS, D_MODEL, SB_WIDTH), D_MODEL ** -0.5),
        'q_norm_b': gain(20, (N_B_LAYERS, SB_HEAD_DIM)),
        'sb_bias': SB_BIAS_INIT + nrm(28, (N_B_LAYERS, SB_HEADS), 0.1),
        'w_out_b': nrm(21, (N_B_LAYERS, SB_WIDTH, D_MODEL), SB_WIDTH ** -0.5),
        'norm_ffn': gain(22, (DEPTH, D_MODEL)),
        'w_gate': nrm(23, (DEPTH, D_MODEL, D_FF), D_MODEL ** -0.5),
        'w_up': nrm(24, (DEPTH, D_MODEL, D_FF), D_MODEL ** -0.5),
        'conv_w': nrm(25, (DEPTH, CONV_W, D_FF), CONV_W ** -0.5),
        'conv_b': nrm(26, (DEPTH, D_FF), 0.01),
        'w_down': nrm(27, (DEPTH, D_FF, D_MODEL), D_FF ** -0.5),
    }


def reference(x_prompt, x_sample, cache_k, cache_v, page_table, state_mlstm_C, state_mlstm_n, state_mlstm_m, state_conv,
              norm_mix_a, w_in_a, b_igate, b_fgate, headnorm_a, w_out_a, norm_kv, w_kv, k_norm,
              norm_mix_b, w_q_b, q_norm_b, sb_bias, w_out_b, norm_ffn, w_gate, w_up, conv_w, conv_b, w_down):
    p = dict(norm_mix_a=norm_mix_a, w_in_a=w_in_a, b_igate=b_igate, b_fgate=b_fgate, headnorm_a=headnorm_a,
             w_out_a=w_out_a, norm_kv=norm_kv, w_kv=w_kv, k_norm=k_norm, norm_mix_b=norm_mix_b, w_q_b=w_q_b,
             q_norm_b=q_norm_b, sb_bias=sb_bias, w_out_b=w_out_b, norm_ffn=norm_ffn, w_gate=w_gate, w_up=w_up,
             conv_w=conv_w, conv_b=conv_b, w_down=w_down)
    bp = x_prompt.shape[0]
    empty = jnp.zeros((bp, 0, SB_HEADS, SB_HEAD_DIM), cache_k.dtype)
    c0p = jnp.zeros((N_A_LAYERS, bp) + state_mlstm_C.shape[2:], state_mlstm_C.dtype)
    n0p = jnp.zeros((N_A_LAYERS, bp) + state_mlstm_n.shape[2:], state_mlstm_n.dtype)
    m0p = jnp.zeros((N_A_LAYERS, bp) + state_mlstm_m.shape[2:], state_mlstm_m.dtype)
    conv0p = jnp.zeros((DEPTH, bp) + state_conv.shape[2:], state_conv.dtype)
    y_prompt, k_p, v_p, c_p, n_p, m_p, conv_p = trunk(x_prompt, empty, empty, c0p, n0p, m0p, conv0p, p)
    db, n_pages = page_table.shape
    past_k = cache_k[page_table].reshape(db, n_pages * PAGE_SIZE, SB_HEADS, SB_HEAD_DIM)
    past_v = cache_v[page_table].reshape(db, n_pages * PAGE_SIZE, SB_HEADS, SB_HEAD_DIM)
    y_sample, k_s, v_s, c_s, n_s, m_s, conv_s = trunk(x_sample, past_k, past_v, state_mlstm_C, state_mlstm_n,
                                                      state_mlstm_m, state_conv, p)
    return (y_prompt, y_sample, k_p, v_p, c_p, n_p, m_p, conv_p, k_s, v_s, c_s, n_s, m_s, conv_s)
```

```python
import functools

import jax
import jax.numpy as jnp
from jax import lax
from jax.experimental import pallas as pl
from jax.experimental.pallas import tpu as pltpu

F32 = jnp.float32
EPS = 1e-6
MLSTM_CHUNK = 64
MIB = 1024 * 1024
LANES = 128
SUBLANES = 8


def _params(semantics, vmem_mib):
    return pltpu.CompilerParams(dimension_semantics=semantics, vmem_limit_bytes=vmem_mib * MIB)


def _rms(x):
    return x * lax.rsqrt(jnp.mean(x * x, axis=-1, keepdims=True) + EPS)


def _log_sigmoid(x):
    return jnp.minimum(x, 0.0) - jnp.log1p(jnp.exp(-jnp.abs(x)))


def _norm_matmul_kernel(*refs, head_dim, n_head_tiles, has_small):
    if has_small:
        x_ref, g_ref, w_ref, hg_ref, ws_ref, o_ref, os_ref, h_scr = refs
    else:
        x_ref, g_ref, w_ref, hg_ref, o_ref, h_scr = refs
    j = pl.program_id(1)

    @pl.when(j == 0)
    def _():
        h = _rms(x_ref[...]) * g_ref[...]
        h_scr[...] = h
        if has_small:
            os_ref[...] = jnp.dot(h, ws_ref[...], preferred_element_type=F32)

    y = jnp.dot(h_scr[...], w_ref[...], preferred_element_type=F32)
    if n_head_tiles == 0:
        o_ref[...] = y
    else:
        tn = y.shape[1]

        @pl.when(j < n_head_tiles)
        def _():
            for c in range(tn // head_dim):
                sl = slice(c * head_dim, (c + 1) * head_dim)
                o_ref[:, sl] = _rms(y[:, sl]) * hg_ref[...]

        @pl.when(j >= n_head_tiles)
        def _():
            o_ref[...] = y


def norm_matmul(x, g, w, *, n_out, tm, tn, head_gain=None, n_head_cols=0, w_small=None):
    m, k = x.shape
    head_dim = LANES if head_gain is None else head_gain.shape[-1]
    hg = jnp.ones((1, head_dim), F32) if head_gain is None else head_gain.reshape(1, head_dim)
    assert m % tm == 0 and n_out % tn == 0 and n_head_cols % tn == 0 and tn % head_dim == 0
    has_small = w_small is not None
    in_specs = [
        pl.BlockSpec((tm, k), lambda i, j: (i, 0)),
        pl.BlockSpec((1, k), lambda i, j: (0, 0)),
        pl.BlockSpec((k, tn), lambda i, j: (0, j)),
        pl.BlockSpec((1, head_dim), lambda i, j: (0, 0)),
    ]
    args = [x, g.reshape(1, k), w, hg]
    out_shape = [jax.ShapeDtypeStruct((m, n_out), F32)]
    out_specs = [pl.BlockSpec((tm, tn), lambda i, j: (i, j))]
    if has_small:
        ns = w_small.shape[1]
        in_specs.append(pl.BlockSpec((k, ns), lambda i, j: (0, 0)))
        args.append(w_small)
        out_shape.append(jax.ShapeDtypeStruct((m, ns), F32))
        out_specs.append(pl.BlockSpec((tm, ns), lambda i, j: (i, 0)))
    outs = pl.pallas_call(
        functools.partial(_norm_matmul_kernel, head_dim=head_dim, n_head_tiles=n_head_cols // tn,
                          has_small=has_small),
        grid=(m // tm, n_out // tn),
        in_specs=in_specs,
        out_specs=out_specs,
        out_shape=out_shape,
        scratch_shapes=[pltpu.VMEM((tm, k), F32)],
        compiler_params=_params(("arbitrary", "arbitrary"), 48),
        name="norm_matmul",
    )(*args)
    return outs if has_small else outs[0]


def _matmul_res_kernel(a_ref, w_ref, r_ref, o_ref):
    o_ref[...] = r_ref[...] + jnp.dot(a_ref[...], w_ref[...], preferred_element_type=F32)


def matmul_residual(a, w, res, *, tm, tn):
    m, k = a.shape
    n = w.shape[1]
    assert m % tm == 0 and n % tn == 0
    return pl.pallas_call(
        _matmul_res_kernel,
        grid=(m // tm, n // tn),
        in_specs=[
            pl.BlockSpec((tm, k), lambda i, j: (i, 0)),
            pl.BlockSpec((k, tn), lambda i, j: (0, j)),
            pl.BlockSpec((tm, tn), lambda i, j: (i, j)),
        ],
        out_specs=pl.BlockSpec((tm, tn), lambda i, j: (i, j)),
        out_shape=jax.ShapeDtypeStruct((m, n), F32),
        compiler_params=_params(("arbitrary", "arbitrary"), 48),
        name="matmul_residual",
    )(a, w, res)


def _ffn_kernel(*refs, seq_tiles, per_row_state):
    if per_row_state:
        (x_ref, g_ref, wg_ref, wu_ref, wd_ref, cw_ref, cb_ref, p2_ref, p1_ref,
         o_ref, tail_ref, h_scr) = refs
    else:
        (x_ref, g_ref, wg_ref, wu_ref, wd_ref, cw_ref, cb_ref,
         o_ref, tail_ref, h_scr, carry_scr) = refs
    i = pl.program_id(0)
    f = pl.program_id(1)

    @pl.when(f == 0)
    def _():
        h_scr[...] = _rms(x_ref[...]) * g_ref[...]

    h = h_scr[...]
    u = jnp.dot(h, wg_ref[...], preferred_element_type=F32)
    up = jnp.dot(h, wu_ref[...], preferred_element_type=F32)
    tm = u.shape[0]
    if per_row_state:
        u1 = p1_ref[...]
        u2 = p2_ref[...]
        tail_ref[...] = u
    else:
        @pl.when(i % seq_tiles == 0)
        def _():
            carry_scr[f] = jnp.zeros(carry_scr.shape[1:], F32)

        prev = carry_scr[f]
        row = lax.broadcasted_iota(jnp.int32, (tm, 1), 0)
        u1 = jnp.where(row == 0, prev[1:2], pltpu.roll(u, 1, axis=0))
        u2 = jnp.where(row == 0, prev[0:1], jnp.where(row == 1, prev[1:2], pltpu.roll(u, 2, axis=0)))
        last2 = u[tm - 2:tm]
        tail_ref[0] = last2
        carry_scr[f, 0:2, :] = last2
    conv = cb_ref[...] + cw_ref[0:1] * u2 + cw_ref[1:2] * u1 + cw_ref[2:3] * u
    act = conv * jax.nn.sigmoid(conv) * up
    y = jnp.dot(act, wd_ref[...], preferred_element_type=F32)

    @pl.when(f == 0)
    def _():
        o_ref[...] = x_ref[...] + y

    @pl.when(f > 0)
    def _():
        o_ref[...] += y


def conv_ffn(x, g, w_gate, w_up, w_down, conv_w, conv_b, *, tm, tf, seq_len=None, state=None):
    m, d = x.shape
    nf = w_gate.shape[1]
    assert m % tm == 0 and nf % tf == 0
    per_row_state = state is not None
    grid = (m // tm, nf // tf)
    row_spec = pl.BlockSpec((tm, d), lambda i, f: (i, 0), pipeline_mode=pl.Buffered(1))
    in_specs = [
        row_spec,
        pl.BlockSpec((1, d), lambda i, f: (0, 0)),
        pl.BlockSpec((d, tf), lambda i, f: (0, f)),
        pl.BlockSpec((d, tf), lambda i, f: (0, f)),
        pl.BlockSpec((tf, d), lambda i, f: (f, 0)),
        pl.BlockSpec((3, tf), lambda i, f: (0, f)),
        pl.BlockSpec((1, tf), lambda i, f: (0, f)),
    ]
    args = [x, g.reshape(1, d), w_gate, w_up, w_down, conv_w, conv_b.reshape(1, nf)]
    scratch = [pltpu.VMEM((tm, d), F32)]
    if per_row_state:
        in_specs += [pl.BlockSpec((tm, tf), lambda i, f: (i, f))] * 2
        args += [state[0], state[1]]
        tail_shape = jax.ShapeDtypeStruct((m, nf), F32)
        tail_spec = pl.BlockSpec((tm, tf), lambda i, f: (i, f))
        seq_tiles = 1
    else:
        assert seq_len % tm == 0
        seq_tiles = seq_len // tm
        tail_shape = jax.ShapeDtypeStruct((m // tm, 2, nf), F32)
        tail_spec = pl.BlockSpec((1, 2, tf), lambda i, f: (i, 0, f))
        scratch.append(pltpu.VMEM((nf // tf, SUBLANES, tf), F32))
    return pl.pallas_call(
        functools.partial(_ffn_kernel, seq_tiles=seq_tiles, per_row_state=per_row_state),
        grid=grid,
        in_specs=in_specs,
        out_specs=[row_spec, tail_spec],
        out_shape=[jax.ShapeDtypeStruct((m, d), F32), tail_shape],
        scratch_shapes=scratch,
        compiler_params=_params(("arbitrary", "arbitrary"), 56),
        name="conv_ffn_rows" if per_row_state else "conv_ffn_seq",
    )(*args)


def _mlstm_seq_kernel(q_ref, k_ref, v_ref, o_ref, gt_ref, bias_ref, hn_ref,
                      hg_ref, c_ref, n_ref, m_ref, *, heads, dk, dv, chunk):
    c = pl.program_id(1)

    @pl.when(c == 0)
    def _():
        c_ref[...] = jnp.zeros_like(c_ref)
        n_ref[...] = jnp.zeros_like(n_ref)
        m_ref[...] = jnp.zeros_like(m_ref)

    gates = gt_ref[0] + bias_ref[...]
    ig_all = gates[:, :heads]
    lf_all = _log_sigmoid(gates[:, heads:])
    ti = lax.broadcasted_iota(jnp.int32, (chunk, chunk), 0)
    si = lax.broadcasted_iota(jnp.int32, (chunk, chunk), 1)
    eye = ti == si
    lower = si <= ti
    for h in range(heads):
        qs = slice(h * dk, (h + 1) * dk)
        vs = slice(h * dv, (h + 1) * dv)
        q = q_ref[0, :, qs]
        k = k_ref[0, :, qs] * (dk ** -0.5)
        v = v_ref[0, :, vs]
        ig_c = ig_all[:, h:h + 1]
        lf_c = lf_all[:, h:h + 1]
        lf_r = jnp.sum(jnp.where(eye, lf_c, 0.0), axis=0, keepdims=True)
        ig_r = jnp.sum(jnp.where(eye, ig_c, 0.0), axis=0, keepdims=True)
        b_c = jnp.sum(jnp.where(lower, lf_r, 0.0), axis=1, keepdims=True)
        b_r = jnp.sum(jnp.where(ti <= si, lf_c, 0.0), axis=0, keepdims=True)
        d = jnp.where(lower, b_c - b_r + ig_r, -jnp.inf)
        m_prev = m_ref[0, h:h + 1, 0:1]
        inter = b_c + m_prev
        m_row = jnp.maximum(inter, jnp.max(d, axis=1, keepdims=True))
        qk = lax.dot_general(q, k, (((1,), (1,)), ((), ())), preferred_element_type=F32)
        w = jnp.exp(d - m_row) * qk
        w_inter = jnp.exp(inter - m_row)
        c_h = c_ref[0, h]
        n_h = n_ref[0, h:h + 1, :]
        num = (jnp.dot(w, v, preferred_element_type=F32)
               + w_inter * jnp.dot(q, c_h, preferred_element_type=F32))
        den = jnp.sum(w, axis=1, keepdims=True) + w_inter * jnp.sum(q * n_h, axis=1, keepdims=True)
        den = jnp.maximum(jnp.abs(den), jnp.exp(-m_row))
        hc = num / den
        g = b_c[chunk - 1:chunk, :]
        log_ws = g - b_c + ig_c
        m_new = jnp.maximum(g + m_prev, jnp.max(log_ws, axis=0, keepdims=True))
        ws = jnp.exp(log_ws - m_new)
        decay = jnp.exp(g + m_prev - m_new)
        c_ref[0, h] = decay * c_h + lax.dot_general(k, ws * v, (((0,), (0,)), ((), ())),
                                                    preferred_element_type=F32)
        n_ref[0, h:h + 1, :] = decay * n_h + jnp.sum(ws * k, axis=0, keepdims=True)
        m_ref[0, h:h + 1, :] = jnp.broadcast_to(m_new, (1, m_ref.shape[2]))
        hn = _rms(hc) * hn_ref[:, vs]
        hg_ref[0, :, vs] = hn * jax.nn.sigmoid(o_ref[0, :, vs])


def mlstm_seq(p, gates, bias, headnorm, *, heads, dk, dv):
    bsz, s, _ = p.shape
    chunk = MLSTM_CHUNK if s % MLSTM_CHUNK == 0 else s
    nc = s // chunk
    wq = heads * dk
    wv = heads * dv
    assert wv == 2 * wq
    return pl.pallas_call(
        functools.partial(_mlstm_seq_kernel, heads=heads, dk=dk, dv=dv, chunk=chunk),
        grid=(bsz, nc),
        in_specs=[
            pl.BlockSpec((1, chunk, wq), lambda b, c: (b, c, 0)),
            pl.BlockSpec((1, chunk, wq), lambda b, c: (b, c, 1)),
            pl.BlockSpec((1, chunk, wv), lambda b, c: (b, c, 1)),
            pl.BlockSpec((1, chunk, wv), lambda b, c: (b, c, 2)),
            pl.BlockSpec((1, chunk, 2 * heads), lambda b, c: (b, c, 0)),
            pl.BlockSpec((1, 2 * heads), lambda b, c: (0, 0)),
            pl.BlockSpec((1, wv), lambda b, c: (0, 0)),
        ],
        out_specs=[
            pl.BlockSpec((1, chunk, wv), lambda b, c: (b, c, 0)),
            pl.BlockSpec((1, heads, dk, dv), lambda b, c: (b, 0, 0, 0)),
            pl.BlockSpec((1, heads, dk), lambda b, c: (b, 0, 0)),
            pl.BlockSpec((1, heads, dk), lambda b, c: (b, 0, 0)),
        ],
        out_shape=[
            jax.ShapeDtypeStruct((bsz, s, wv), F32),
            jax.ShapeDtypeStruct((bsz, heads, dk, dv), F32),
            jax.ShapeDtypeStruct((bsz, heads, dk), F32),
            jax.ShapeDtypeStruct((bsz, heads, dk), F32),
        ],
        compiler_params=_params(("arbitrary", "arbitrary"), 32),
        name="mlstm_seq",
    )(p, p, p, p, gates, bias, headnorm)


def _mlstm_step_kernel(q_ref, k_ref, v_ref, o_ref, gt_ref, bias_ref, hn_ref, c0_ref, n0_ref, m0_ref,
                       hg_ref, c_ref, n_ref, m_ref, *, heads, dk, dv):
    gates = gt_ref[0] + bias_ref[...]
    ig_all = gates[:, :heads]
    lf_all = _log_sigmoid(gates[:, heads:])
    ri = lax.broadcasted_iota(jnp.int32, (dk, dk), 0)
    ci = lax.broadcasted_iota(jnp.int32, (dk, dk), 1)
    eye = ri == ci
    for h in range(heads):
        qs = slice(h * dk, (h + 1) * dk)
        vs = slice(h * dv, (h + 1) * dv)
        q = q_ref[0, :, qs]
        k = k_ref[0, :, qs] * (dk ** -0.5)
        v = v_ref[0, :, vs]
        ig = ig_all[:, h:h + 1]
        lf = lf_all[:, h:h + 1]
        m_prev = m0_ref[0, h:h + 1, 0:1]
        c_h = c0_ref[0, h]
        n_h = n0_ref[0, h:h + 1, :]
        inter = lf + m_prev
        m_row = jnp.maximum(inter, ig)
        w = jnp.exp(ig - m_row) * jnp.sum(q * k, axis=1, keepdims=True)
        w_inter = jnp.exp(inter - m_row)
        q_rows = jnp.broadcast_to(q, (SUBLANES, dk))
        qc = jnp.dot(q_rows, c_h, preferred_element_type=F32)[0:1]
        num = w * v + w_inter * qc
        den = w + w_inter * jnp.sum(q * n_h, axis=1, keepdims=True)
        den = jnp.maximum(jnp.abs(den), jnp.exp(-m_row))
        hc = num / den
        m_new = jnp.maximum(lf + m_prev, ig)
        ws = jnp.exp(ig - m_new)
        decay = jnp.exp(lf + m_prev - m_new)
        k_col = jnp.sum(jnp.where(eye, k, 0.0), axis=1, keepdims=True)
        c_ref[0, h] = decay * c_h + k_col * (ws * v)
        n_ref[0, h:h + 1, :] = decay * n_h + ws * k
        m_ref[0, h:h + 1, :] = jnp.broadcast_to(m_new, (1, m_ref.shape[2]))
        hn = _rms(hc) * hn_ref[:, vs]
        hg_ref[0, :, vs] = hn * jax.nn.sigmoid(o_ref[0, :, vs])


def mlstm_step(p, gates, bias, headnorm, c0, n0, m0b, *, heads, dk, dv):
    bsz = p.shape[0]
    wq = heads * dk
    wv = heads * dv
    state_specs = [
        pl.BlockSpec((1, heads, dk, dv), lambda b: (b, 0, 0, 0)),
        pl.BlockSpec((1, heads, dk), lambda b: (b, 0, 0)),
        pl.BlockSpec((1, heads, dk), lambda b: (b, 0, 0)),
    ]
    return pl.pallas_call(
        functools.partial(_mlstm_step_kernel, heads=heads, dk=dk, dv=dv),
        grid=(bsz,),
        in_specs=[
            pl.BlockSpec((1, 1, wq), lambda b: (b, 0, 0)),
            pl.BlockSpec((1, 1, wq), lambda b: (b, 0, 1)),
            pl.BlockSpec((1, 1, wv), lambda b: (b, 0, 1)),
            pl.BlockSpec((1, 1, wv), lambda b: (b, 0, 2)),
            pl.BlockSpec((1, 1, 2 * heads), lambda b: (b, 0, 0)),
            pl.BlockSpec((1, 2 * heads), lambda b: (0, 0)),
            pl.BlockSpec((1, wv), lambda b: (0, 0)),
        ] + state_specs,
        out_specs=[pl.BlockSpec((1, 1, wv), lambda b: (b, 0, 0))] + state_specs,
        out_shape=[
            jax.ShapeDtypeStruct((bsz, 1, wv), F32),
            jax.ShapeDtypeStruct(c0.shape, F32),
            jax.ShapeDtypeStruct(n0.shape, F32),
            jax.ShapeDtypeStruct(n0.shape, F32),
        ],
        compiler_params=_params(("arbitrary",), 32),
        name="mlstm_step",
    )(p, p, p, p, gates, bias, headnorm, c0, n0, m0b)


def _softplus_pair(z):
    l = jnp.log1p(jnp.exp(-jnp.abs(z)))
    return jnp.maximum(z, 0.0) + l, jnp.minimum(z, 0.0) - l


def _sb_seq_kernel(bias_ref, q_ref, k_ref, v_ref, o_ref, *, tq, scale):
    h = pl.program_id(1)
    qi = pl.program_id(2)
    q = q_ref[0]
    bias = bias_ref[h]
    ji = lax.broadcasted_iota(jnp.int32, (tq, tq), 0)
    si = lax.broadcasted_iota(jnp.int32, (tq, tq), 1)
    later = jnp.where(ji > si, 1.0, 0.0).astype(F32)

    def body(step, carry):
        acc, run = carry
        kb = qi - step
        off = pl.multiple_of(kb * tq, tq)
        kk = k_ref[0, pl.ds(off, tq), :]
        vv = v_ref[0, pl.ds(off, tq), :]
        z = lax.dot_general(q, kk, (((1,), (1,)), ((), ())), preferred_element_type=F32) * scale + bias
        sp, ls = _softplus_pair(z)
        valid = si < ji + jnp.where(step > 0, tq, 0)
        sp = jnp.where(valid, sp, 0.0)
        after = jnp.dot(sp, later, preferred_element_type=F32) + run
        a = jnp.where(valid, jnp.exp(ls - after), 0.0)
        acc = acc + jnp.dot(a, vv, preferred_element_type=F32)
        run = run + jnp.sum(sp, axis=1, keepdims=True)
        return acc, run

    acc, _ = lax.fori_loop(0, qi + 1, body,
                           (jnp.zeros(q.shape, F32), jnp.zeros((tq, 1), F32)))
    o_ref[0] = acc


def sb_attention_seq(q, k, v, bias, *, heads, tq):
    bsz, s, width = q.shape
    hd = width // heads
    assert s % tq == 0
    return pl.pallas_call(
        functools.partial(_sb_seq_kernel, tq=tq, scale=hd ** -0.5),
        grid=(bsz, heads, s // tq),
        in_specs=[
            pl.BlockSpec(memory_space=pltpu.SMEM),
            pl.BlockSpec((1, tq, hd), lambda b, h, i: (b, i, h)),
            pl.BlockSpec((1, s, hd), lambda b, h, i: (b, 0, h)),
            pl.BlockSpec((1, s, hd), lambda b, h, i: (b, 0, h)),
        ],
        out_specs=pl.BlockSpec((1, tq, hd), lambda b, h, i: (b, i, h)),
        out_shape=jax.ShapeDtypeStruct((bsz, s, width), F32),
        compiler_params=_params(("arbitrary", "arbitrary", "arbitrary"), 32),
        name="sb_attention_seq",
    )(bias, q, k, v)


def _sb_paged_kernel(pt_ref, bias_ref, q_ref, k_ref, v_ref, o_ref, acc_scr, run_scr, *, heads, page, scale):
    p = pl.program_id(1)

    @pl.when(p == 0)
    def _():
        acc_scr[...] = jnp.zeros_like(acc_scr)
        run_scr[...] = jnp.zeros_like(run_scr)

    q = q_ref[0]
    hrow = lax.broadcasted_iota(jnp.int32, (heads, 1), 0)
    z = jnp.zeros((heads, page), F32)
    for h in range(heads):
        kh = k_ref[0, pl.ds(h, page, stride=heads), :]
        r = lax.dot_general(q, kh, (((1,), (1,)), ((), ())), preferred_element_type=F32)
        z = jnp.where(hrow == h, r, z)
    z = z * scale + bias_ref[...]
    sp, ls = _softplus_pair(z)
    ji = lax.broadcasted_iota(jnp.int32, (page, page), 0)
    si = lax.broadcasted_iota(jnp.int32, (page, page), 1)
    later = jnp.where(ji > si, 1.0, 0.0).astype(F32)
    after = jnp.dot(sp, later, preferred_element_type=F32) + run_scr[...]
    a = jnp.exp(ls - after)
    run_scr[...] += jnp.sum(sp, axis=1, keepdims=True)
    acc = acc_scr[...]
    for h in range(heads):
        vh = v_ref[0, pl.ds(h, page, stride=heads), :]
        r = jnp.dot(a, vh, preferred_element_type=F32)
        acc = acc + jnp.where(hrow == h, r, 0.0)
    acc_scr[...] = acc

    @pl.when(p == pl.num_programs(1) - 1)
    def _():
        o_ref[0] = acc


def sb_attention_paged(q, cache_k, cache_v, page_table, bias):
    bsz, heads, hd = q.shape
    n_pool, page = cache_k.shape[:2]
    n_pages = page_table.shape[1]
    ck = cache_k.reshape(n_pool, page * heads, hd)
    cv = cache_v.reshape(n_pool, page * heads, hd)

    def kv_map(b, p, pt):
        return (pt[b, n_pages - 1 - p], 0, 0)

    return pl.pallas_call(
        functools.partial(_sb_paged_kernel, heads=heads, page=page, scale=hd ** -0.5),
        grid_spec=pltpu.PrefetchScalarGridSpec(
            num_scalar_prefetch=1,
            grid=(bsz, n_pages),
            in_specs=[
                pl.BlockSpec((heads, 1), lambda b, p, pt: (0, 0)),
                pl.BlockSpec((1, heads, hd), lambda b, p, pt: (b, 0, 0)),
                pl.BlockSpec((1, page * heads, hd), kv_map),
                pl.BlockSpec((1, page * heads, hd), kv_map),
            ],
            out_specs=pl.BlockSpec((1, heads, hd), lambda b, p, pt: (b, 0, 0)),
            scratch_shapes=[pltpu.VMEM((heads, hd), F32), pltpu.VMEM((heads, 1), F32)],
        ),
        out_shape=jax.ShapeDtypeStruct((bsz, heads, hd), F32),
        compiler_params=_params(("arbitrary", "arbitrary"), 32),
        name="sb_attention_paged",
    )(page_table, bias.reshape(heads, 1), q, ck, cv)


def _trunk(x, prm, *, seq_len, tm, mlstm_state=None, conv_state=None, paged=None):
    m, d = x.shape
    heads_a, dv = prm["headnorm_a"].shape[1:]
    dk = (prm["w_in_a"].shape[2] - 2 * heads_a * dv - 2 * heads_a) // (2 * heads_a)
    heads_b = prm["sb_bias"].shape[1]
    hd = prm["k_norm"].shape[0]
    wide = 2 * heads_a * dk + 2 * heads_a * dv
    is_seq = seq_len is not None
    bsz = m // seq_len if is_seq else m

    w_in = prm["w_in_a"][0]
    p, gates = norm_matmul(x, prm["norm_mix_a"][0], w_in, n_out=wide, tm=tm, tn=512,
                           w_small=w_in[:, wide:])
    gate_bias = jnp.concatenate([prm["b_igate"][0], prm["b_fgate"][0]]).reshape(1, 2 * heads_a)
    headnorm = prm["headnorm_a"][0].reshape(1, heads_a * dv)
    if is_seq:
        hg, c_new, n_new, m_new = mlstm_seq(p.reshape(bsz, seq_len, wide), gates.reshape(bsz, seq_len, -1),
                                            gate_bias, headnorm, heads=heads_a, dk=dk, dv=dv)
    else:
        c0, n0, m0 = mlstm_state
        m0b = jnp.broadcast_to(m0[..., None], n0.shape)
        hg, c_new, n_new, m_new = mlstm_step(p.reshape(m, 1, wide), gates.reshape(m, 1, -1), gate_bias,
                                             headnorm, c0, n0, m0b, heads=heads_a, dk=dk, dv=dv)
    m_new = m_new[:, :, 0]
    x = matmul_residual(hg.reshape(m, heads_a * dv), prm["w_out_a"][0], x, tm=tm, tn=512)

    def ffn(x, layer):
        a = (x, prm["norm_ffn"][layer], prm["w_gate"][layer], prm["w_up"][layer], prm["w_down"][layer],
             prm["conv_w"][layer], prm["conv_b"][layer])
        if is_seq:
            y, tails = conv_ffn(*a, tm=tm, tf=256, seq_len=seq_len)
            st = seq_len // tm
            return y, tails[st - 1::st]
        buf = conv_state[layer]
        y, u = conv_ffn(*a, tm=tm, tf=256, state=(buf[:, 0], buf[:, 1]))
        return y, jnp.stack([buf[:, 1], u], axis=1)

    x, conv0 = ffn(x, 0)

    kv = norm_matmul(x, prm["norm_kv"], prm["w_kv"], n_out=2 * heads_b * hd, tm=tm, tn=512,
                     head_gain=prm["k_norm"], n_head_cols=heads_b * hd)
    k_new = kv[:, :heads_b * hd]
    v_new = kv[:, heads_b * hd:]

    q = norm_matmul(x, prm["norm_mix_b"][0], prm["w_q_b"][0], n_out=heads_b * hd, tm=tm, tn=512,
                    head_gain=prm["q_norm_b"][0], n_head_cols=heads_b * hd)
    if is_seq:
        o = sb_attention_seq(q.reshape(bsz, seq_len, -1), k_new.reshape(bsz, seq_len, -1),
                             v_new.reshape(bsz, seq_len, -1), prm["sb_bias"][0], heads=heads_b, tq=256)
    else:
        cache_k, cache_v, page_table = paged
        o = sb_attention_paged(q.reshape(m, heads_b, hd), cache_k, cache_v, page_table, prm["sb_bias"][0])
    x = matmul_residual(o.reshape(m, heads_b * hd), prm["w_out_b"][0], x, tm=tm, tn=512)
    x, conv1 = ffn(x, 1)
    return (x, k_new.reshape(bsz, -1, heads_b, hd), v_new.reshape(bsz, -1, heads_b, hd),
            c_new[None], n_new[None], m_new[None], jnp.stack([conv0, conv1]))


def kernel(x_prompt, x_sample, cache_k, cache_v, page_table, state_mlstm_C, state_mlstm_n, state_mlstm_m, state_conv, norm_mix_a, w_in_a, b_igate, b_fgate, headnorm_a, w_out_a, norm_kv, w_kv, k_norm, norm_mix_b, w_q_b, q_norm_b, sb_bias, w_out_b, norm_ffn, w_gate, w_up, conv_w, conv_b, w_down):
    prm = dict(norm_mix_a=norm_mix_a, w_in_a=w_in_a, b_igate=b_igate, b_fgate=b_fgate, headnorm_a=headnorm_a,
               w_out_a=w_out_a, norm_kv=norm_kv, w_kv=w_kv, k_norm=k_norm, norm_mix_b=norm_mix_b, w_q_b=w_q_b,
               q_norm_b=q_norm_b, sb_bias=sb_bias, w_out_b=w_out_b, norm_ffn=norm_ffn, w_gate=w_gate, w_up=w_up,
               conv_w=conv_w, conv_b=conv_b, w_down=w_down)
    bp, s, d = x_prompt.shape
    yp, kp, vp, cp, np_, mp, convp = _trunk(x_prompt.reshape(bp * s, d), prm, seq_len=s, tm=min(1024, s))
    db, ds, _ = x_sample.shape
    assert ds == 1
    ys, ks, vs, cs, ns, ms, convs = _trunk(
        x_sample.reshape(db, d), prm, seq_len=None, tm=db,
        mlstm_state=(state_mlstm_C[0], state_mlstm_n[0], state_mlstm_m[0]),
        conv_state=state_conv, paged=(cache_k, cache_v, page_table))
    return (yp.reshape(bp, s, d), ys.reshape(db, 1, d), kp, vp, cp, np_, mp, convp,
            ks, vs, cs, ns, ms, convs)
```

```python
import functools

import jax
import jax.numpy as jnp
from jax import lax
from jax.experimental import pallas as pl
from jax.experimental.pallas import tpu as pltpu

F32 = jnp.float32
EPS = 1e-6
MLSTM_CHUNK = 64
MIB = 1024 * 1024
LANES = 128
SUBLANES = 8
PAGES_PER_STEP = 8
FFN_SUB_ROWS = 512


def _params(semantics, vmem_mib):
    return pltpu.CompilerParams(dimension_semantics=semantics, vmem_limit_bytes=vmem_mib * MIB)


def _rms(x):
    return x * lax.rsqrt(jnp.mean(x * x, axis=-1, keepdims=True) + EPS)


def _log_sigmoid(x):
    return jnp.minimum(x, 0.0) - jnp.log1p(jnp.exp(-jnp.abs(x)))


def _norm_matmul_kernel(*refs, head_dim, tile_bounds, n_normed, has_small):
    n_main = len(tile_bounds) - 1
    x_ref, g_ref, w_ref, hg_ref = refs[:4]
    rest = refs[4:]
    if has_small:
        ws_ref, rest = rest[0], rest[1:]
    o_refs = rest[:n_main]
    os_ref = rest[n_main] if has_small else None
    h_scr = rest[-1]
    j = pl.program_id(1)

    @pl.when(j == 0)
    def _():
        h = _rms(x_ref[...]) * g_ref[...]
        h_scr[...] = h
        if has_small:
            os_ref[...] = jnp.dot(h, ws_ref[...], preferred_element_type=F32)

    y = jnp.dot(h_scr[...], w_ref[...], preferred_element_type=F32)
    tn = y.shape[1]
    for s in range(n_main):
        @pl.when(jnp.logical_and(j >= tile_bounds[s], j < tile_bounds[s + 1]))
        def _(o_ref=o_refs[s], normed=s < n_normed):
            if normed:
                for c in range(tn // head_dim):
                    sl = slice(c * head_dim, (c + 1) * head_dim)
                    o_ref[:, sl] = _rms(y[:, sl]) * hg_ref[...]
            else:
                o_ref[...] = y


def norm_matmul(x, g, w, *, splits, tm, tn, head_gain=None, n_normed=0, w_small=None):
    m, k = x.shape
    head_dim = LANES if head_gain is None else head_gain.shape[-1]
    hg = jnp.ones((1, head_dim), F32) if head_gain is None else head_gain.reshape(1, head_dim)
    assert m % tm == 0 and tn % head_dim == 0 and all(s % tn == 0 for s in splits)
    bounds = [0]
    for s in splits:
        bounds.append(bounds[-1] + s // tn)
    has_small = w_small is not None
    in_specs = [
        pl.BlockSpec((tm, k), lambda i, j: (i, 0)),
        pl.BlockSpec((1, k), lambda i, j: (0, 0)),
        pl.BlockSpec((k, tn), lambda i, j: (0, j)),
        pl.BlockSpec((1, head_dim), lambda i, j: (0, 0)),
    ]
    args = [x, g.reshape(1, k), w, hg]
    if has_small:
        ns = w_small.shape[1]
        in_specs.append(pl.BlockSpec((k, ns), lambda i, j: (0, 0)))
        args.append(w_small)

    def out_map(lo, hi):
        return lambda i, j: (i, jnp.clip(j, lo, hi - 1) - lo)

    out_shape = [jax.ShapeDtypeStruct((m, s), F32) for s in splits]
    out_specs = [pl.BlockSpec((tm, tn), out_map(bounds[s], bounds[s + 1])) for s in range(len(splits))]
    if has_small:
        out_shape.append(jax.ShapeDtypeStruct((m, ns), F32))
        out_specs.append(pl.BlockSpec((tm, ns), lambda i, j: (i, 0)))
    return pl.pallas_call(
        functools.partial(_norm_matmul_kernel, head_dim=head_dim, tile_bounds=tuple(bounds),
                          n_normed=n_normed, has_small=has_small),
        grid=(m // tm, bounds[-1]),
        in_specs=in_specs,
        out_specs=out_specs,
        out_shape=out_shape,
        scratch_shapes=[pltpu.VMEM((tm, k), F32)],
        compiler_params=_params(("arbitrary", "arbitrary"), 48),
        name="norm_matmul",
    )(*args)


def _split_heads_kernel(*refs, heads, hd):
    n = len(refs) // 2
    for x_ref, o_ref in zip(refs[:n], refs[n:]):
        tm = x_ref.shape[0]
        for c in range(heads):
            o_ref[pl.ds(c, tm, stride=heads), :] = x_ref[:, c * hd:(c + 1) * hd]


def split_heads(xs, *, heads, tm):
    m, width = xs[0].shape
    hd = width // heads
    assert m % tm == 0
    outs = pl.pallas_call(
        functools.partial(_split_heads_kernel, heads=heads, hd=hd),
        grid=(m // tm,),
        in_specs=[pl.BlockSpec((tm, width), lambda i: (i, 0))] * len(xs),
        out_specs=[pl.BlockSpec((tm * heads, hd), lambda i: (i, 0))] * len(xs),
        out_shape=[jax.ShapeDtypeStruct((m * heads, hd), F32)] * len(xs),
        compiler_params=_params(("arbitrary",), 32),
        name="split_heads",
    )(*xs)
    return [o.reshape(m, heads, hd) for o in outs]


def _matmul_res_kernel(a_ref, w_ref, r_ref, o_ref):
    o_ref[...] = r_ref[...] + jnp.dot(a_ref[...], w_ref[...], preferred_element_type=F32)


def matmul_residual(a, w, res, *, tm, tn):
    m, k = a.shape
    n = w.shape[1]
    assert m % tm == 0 and n % tn == 0
    return pl.pallas_call(
        _matmul_res_kernel,
        grid=(m // tm, n // tn),
        in_specs=[
            pl.BlockSpec((tm, k), lambda i, j: (i, 0)),
            pl.BlockSpec((k, tn), lambda i, j: (0, j)),
            pl.BlockSpec((tm, tn), lambda i, j: (i, j)),
        ],
        out_specs=pl.BlockSpec((tm, tn), lambda i, j: (i, j)),
        out_shape=jax.ShapeDtypeStruct((m, n), F32),
        compiler_params=_params(("arbitrary", "arbitrary"), 48),
        name="matmul_residual",
    )(a, w, res)


def _ffn_kernel(*refs, seq_tiles, per_row_state, sub_rows):
    if per_row_state:
        (x_ref, g_ref, wg_ref, wu_ref, wd_ref, cw_ref, cb_ref, p2_ref, p1_ref,
         o_ref, tail_ref, h_scr) = refs
    else:
        (x_ref, g_ref, wg_ref, wu_ref, wd_ref, cw_ref, cb_ref,
         o_ref, tail_ref, h_scr, carry_scr) = refs
    i = pl.program_id(0)
    f = pl.program_id(1)

    @pl.when(f == 0)
    def _():
        x = x_ref[...]
        h_scr[...] = _rms(x) * g_ref[...]
        o_ref[...] = x

    tm = x_ref.shape[0]
    sub = min(tm, sub_rows)
    if not per_row_state:
        @pl.when(i % seq_tiles == 0)
        def _():
            carry_scr[f] = jnp.zeros(carry_scr.shape[1:], F32)

        prev = carry_scr[f, 0:2, :]
        row = lax.broadcasted_iota(jnp.int32, (sub, 1), 0)
    for s in range(tm // sub):
        rows = slice(s * sub, (s + 1) * sub)
        h = h_scr[rows, :]
        u = jnp.dot(h, wg_ref[...], preferred_element_type=F32)
        up = jnp.dot(h, wu_ref[...], preferred_element_type=F32)
        if per_row_state:
            u1 = p1_ref[rows, :]
            u2 = p2_ref[rows, :]
            tail_ref[rows, :] = u
        else:
            u1 = jnp.where(row == 0, prev[1:2], pltpu.roll(u, 1, axis=0))
            u2 = jnp.where(row == 0, prev[0:1], jnp.where(row == 1, prev[1:2], pltpu.roll(u, 2, axis=0)))
            prev = u[sub - 2:sub]
        conv = cb_ref[...] + cw_ref[0:1] * u2 + cw_ref[1:2] * u1 + cw_ref[2:3] * u
        act = conv * jax.nn.sigmoid(conv) * up
        o_ref[rows, :] += jnp.dot(act, wd_ref[...], preferred_element_type=F32)
    if not per_row_state:
        tail_ref[0] = prev
        carry_scr[f, 0:2, :] = prev


def conv_ffn(x, g, w_gate, w_up, w_down, conv_w, conv_b, *, layer, tm, tf, seq_len=None, state=None):
    m, d = x.shape
    nf = w_gate.shape[2]
    assert m % tm == 0 and nf % tf == 0
    per_row_state = state is not None
    grid = (m // tm, nf // tf)
    row_spec = pl.BlockSpec((tm, d), lambda i, f: (i, 0), pipeline_mode=pl.Buffered(1))
    in_specs = [
        row_spec,
        pl.BlockSpec((1, d), lambda i, f: (0, 0)),
        pl.BlockSpec((None, d, tf), lambda i, f: (layer, 0, f)),
        pl.BlockSpec((None, d, tf), lambda i, f: (layer, 0, f)),
        pl.BlockSpec((None, tf, d), lambda i, f: (layer, f, 0)),
        pl.BlockSpec((3, tf), lambda i, f: (0, f)),
        pl.BlockSpec((1, tf), lambda i, f: (0, f)),
    ]
    args = [x, g.reshape(1, d), w_gate, w_up, w_down, conv_w, conv_b.reshape(1, nf)]
    scratch = [pltpu.VMEM((tm, d), F32)]
    if per_row_state:
        in_specs += [pl.BlockSpec((tm, tf), lambda i, f: (i, f))] * 2
        args += [state[0], state[1]]
        tail_shape = jax.ShapeDtypeStruct((m, nf), F32)
        tail_spec = pl.BlockSpec((tm, tf), lambda i, f: (i, f))
        seq_tiles = 1
    else:
        assert seq_len % tm == 0
        seq_tiles = seq_len // tm
        tail_shape = jax.ShapeDtypeStruct((m // tm, 2, nf), F32)
        tail_spec = pl.BlockSpec((1, 2, tf), lambda i, f: (i, 0, f))
        scratch.append(pltpu.VMEM((nf // tf, SUBLANES, tf), F32))
    return pl.pallas_call(
        functools.partial(_ffn_kernel, seq_tiles=seq_tiles, per_row_state=per_row_state,
                          sub_rows=FFN_SUB_ROWS),
        grid=grid,
        in_specs=in_specs,
        out_specs=[row_spec, tail_spec],
        out_shape=[jax.ShapeDtypeStruct((m, d), F32), tail_shape],
        scratch_shapes=scratch,
        compiler_params=_params(("arbitrary", "arbitrary"), 56),
        name="conv_ffn_rows" if per_row_state else "conv_ffn_seq",
    )(*args)


def _mlstm_seq_kernel(q_ref, k_ref, v_ref, o_ref, gt_ref, bias_ref, hn_ref,
                      hg_ref, c_ref, n_ref, m_ref, *, heads, dk, dv, chunk):
    c = pl.program_id(1)

    @pl.when(c == 0)
    def _():
        c_ref[...] = jnp.zeros_like(c_ref)
        n_ref[...] = jnp.zeros_like(n_ref)
        m_ref[...] = jnp.zeros_like(m_ref)

    gates = gt_ref[0] + bias_ref[...]
    ig_all = gates[:, :heads]
    lf_all = _log_sigmoid(gates[:, heads:])
    ti = lax.broadcasted_iota(jnp.int32, (chunk, chunk), 0)
    si = lax.broadcasted_iota(jnp.int32, (chunk, chunk), 1)
    eye = ti == si
    lower = si <= ti
    hs = range(heads)
    qsl = [slice(h * dk, (h + 1) * dk) for h in hs]
    vsl = [slice(h * dv, (h + 1) * dv) for h in hs]
    q = [q_ref[0, :, qsl[h]] for h in hs]
    k = [k_ref[0, :, qsl[h]] * (dk ** -0.5) for h in hs]
    v = [v_ref[0, :, vsl[h]] for h in hs]
    ig_c = [ig_all[:, h:h + 1] for h in hs]
    lf_c = [lf_all[:, h:h + 1] for h in hs]
    m_prev = [m_ref[0, h:h + 1, 0:1] for h in hs]
    n_h = [n_ref[0, h:h + 1, :] for h in hs]
    lf_r = [jnp.sum(jnp.where(eye, lf_c[h], 0.0), axis=0, keepdims=True) for h in hs]
    ig_r = [jnp.sum(jnp.where(eye, ig_c[h], 0.0), axis=0, keepdims=True) for h in hs]
    b_r = [jnp.sum(jnp.where(ti <= si, lf_c[h], 0.0), axis=0, keepdims=True) for h in hs]
    b_c = [jnp.sum(jnp.where(lower, lf_r[h], 0.0), axis=1, keepdims=True) for h in hs]
    d = [jnp.where(lower, b_c[h] - b_r[h] + ig_r[h], -jnp.inf) for h in hs]
    d_max = [jnp.max(d[h], axis=1, keepdims=True) for h in hs]
    inter = [b_c[h] + m_prev[h] for h in hs]
    m_row = [jnp.maximum(inter[h], d_max[h]) for h in hs]
    qk = [lax.dot_general(q[h], k[h], (((1,), (1,)), ((), ())), preferred_element_type=F32) for h in hs]
    w = [jnp.exp(d[h] - m_row[h]) * qk[h] for h in hs]
    w_inter = [jnp.exp(inter[h] - m_row[h]) for h in hs]
    w_sum = [jnp.sum(w[h], axis=1, keepdims=True) for h in hs]
    qn = [jnp.sum(q[h] * n_h[h], axis=1, keepdims=True) for h in hs]
    num = [jnp.dot(w[h], v[h], preferred_element_type=F32)
           + w_inter[h] * jnp.dot(q[h], c_ref[0, h], preferred_element_type=F32) for h in hs]
    den = [jnp.maximum(jnp.abs(w_sum[h] + w_inter[h] * qn[h]), jnp.exp(-m_row[h])) for h in hs]
    hc = [num[h] / den[h] for h in hs]
    hc_ms = [jnp.mean(hc[h] * hc[h], axis=-1, keepdims=True) for h in hs]
    for h in hs:
        hn = hc[h] * lax.rsqrt(hc_ms[h] + EPS) * hn_ref[:, vsl[h]]
        hg_ref[0, :, vsl[h]] = hn * jax.nn.sigmoid(o_ref[0, :, vsl[h]])
    g = [b_c[h][chunk - 1:chunk, :] for h in hs]
    log_ws = [g[h] - b_c[h] + ig_c[h] for h in hs]
    m_new = [jnp.maximum(g[h] + m_prev[h], jnp.max(log_ws[h], axis=0, keepdims=True)) for h in hs]
    ws = [jnp.exp(log_ws[h] - m_new[h]) for h in hs]
    decay = [jnp.exp(g[h] + m_prev[h] - m_new[h]) for h in hs]
    for h in hs:
        c_ref[0, h] = decay[h] * c_ref[0, h] + lax.dot_general(
            k[h], ws[h] * v[h], (((0,), (0,)), ((), ())), preferred_element_type=F32)
        n_ref[0, h:h + 1, :] = decay[h] * n_h[h] + jnp.sum(ws[h] * k[h], axis=0, keepdims=True)
        m_ref[0, h:h + 1, :] = jnp.broadcast_to(m_new[h], (1, m_ref.shape[2]))


def mlstm_seq(p, gates, bias, headnorm, *, heads, dk, dv):
    bsz, s, _ = p.shape
    chunk = MLSTM_CHUNK if s % MLSTM_CHUNK == 0 else s
    nc = s // chunk
    wq = heads * dk
    wv = heads * dv
    assert wv == 2 * wq
    return pl.pallas_call(
        functools.partial(_mlstm_seq_kernel, heads=heads, dk=dk, dv=dv, chunk=chunk),
        grid=(bsz, nc),
        in_specs=[
            pl.BlockSpec((1, chunk, wq), lambda b, c: (b, c, 0)),
            pl.BlockSpec((1, chunk, wq), lambda b, c: (b, c, 1)),
            pl.BlockSpec((1, chunk, wv), lambda b, c: (b, c, 1)),
            pl.BlockSpec((1, chunk, wv), lambda b, c: (b, c, 2)),
            pl.BlockSpec((1, chunk, 2 * heads), lambda b, c: (b, c, 0)),
            pl.BlockSpec((1, 2 * heads), lambda b, c: (0, 0)),
            pl.BlockSpec((1, wv), lambda b, c: (0, 0)),
        ],
        out_specs=[
            pl.BlockSpec((1, chunk, wv), lambda b, c: (b, c, 0)),
            pl.BlockSpec((1, heads, dk, dv), lambda b, c: (b, 0, 0, 0)),
            pl.BlockSpec((1, heads, dk), lambda b, c: (b, 0, 0)),
            pl.BlockSpec((1, heads, dk), lambda b, c: (b, 0, 0)),
        ],
        out_shape=[
            jax.ShapeDtypeStruct((bsz, s, wv), F32),
            jax.ShapeDtypeStruct((bsz, heads, dk, dv), F32),
            jax.ShapeDtypeStruct((bsz, heads, dk), F32),
            jax.ShapeDtypeStruct((bsz, heads, dk), F32),
        ],
        compiler_params=_params(("arbitrary", "arbitrary"), 32),
        name="mlstm_seq",
    )(p, p, p, p, gates, bias, headnorm)


def _mlstm_step_kernel(q_ref, k_ref, v_ref, o_ref, gt_ref, bias_ref, hn_ref, c0_ref, n0_ref, m0_ref,
                       hg_ref, c_ref, n_ref, m_ref, *, heads, dk, dv):
    gates = gt_ref[0] + bias_ref[...]
    ig_all = gates[:, :heads]
    lf_all = _log_sigmoid(gates[:, heads:])
    ri = lax.broadcasted_iota(jnp.int32, (dk, dk), 0)
    ci = lax.broadcasted_iota(jnp.int32, (dk, dk), 1)
    eye = ri == ci
    for h in range(heads):
        qs = slice(h * dk, (h + 1) * dk)
        vs = slice(h * dv, (h + 1) * dv)
        q = q_ref[0, :, qs]
        k = k_ref[0, :, qs] * (dk ** -0.5)
        v = v_ref[0, :, vs]
        ig = ig_all[:, h:h + 1]
        lf = lf_all[:, h:h + 1]
        m_prev = m0_ref[0, h:h + 1, 0:1]
        c_h = c0_ref[0, h]
        n_h = n0_ref[0, h:h + 1, :]
        inter = lf + m_prev
        m_row = jnp.maximum(inter, ig)
        w = jnp.exp(ig - m_row) * jnp.sum(q * k, axis=1, keepdims=True)
        w_inter = jnp.exp(inter - m_row)
        q_rows = jnp.broadcast_to(q, (SUBLANES, dk))
        qc = jnp.dot(q_rows, c_h, preferred_element_type=F32)[0:1]
        num = w * v + w_inter * qc
        den = w + w_inter * jnp.sum(q * n_h, axis=1, keepdims=True)
        den = jnp.maximum(jnp.abs(den), jnp.exp(-m_row))
        hc = num / den
        m_new = jnp.maximum(lf + m_prev, ig)
        ws = jnp.exp(ig - m_new)
        decay = jnp.exp(lf + m_prev - m_new)
        k_col = jnp.sum(jnp.where(eye, k, 0.0), axis=1, keepdims=True)
        c_ref[0, h] = decay * c_h + k_col * (ws * v)
        n_ref[0, h:h + 1, :] = decay * n_h + ws * k
        m_ref[0, h:h + 1, :] = jnp.broadcast_to(m_new, (1, m_ref.shape[2]))
        hn = _rms(hc) * hn_ref[:, vs]
        hg_ref[0, :, vs] = hn * jax.nn.sigmoid(o_ref[0, :, vs])


def mlstm_step(p, gates, bias, headnorm, c0, n0, m0b, *, heads, dk, dv):
    bsz = p.shape[0]
    wq = heads * dk
    wv = heads * dv
    state_specs = [
        pl.BlockSpec((1, heads, dk, dv), lambda b: (b, 0, 0, 0)),
        pl.BlockSpec((1, heads, dk), lambda b: (b, 0, 0)),
        pl.BlockSpec((1, heads, dk), lambda b: (b, 0, 0)),
    ]
    return pl.pallas_call(
        functools.partial(_mlstm_step_kernel, heads=heads, dk=dk, dv=dv),
        grid=(bsz,),
        in_specs=[
            pl.BlockSpec((1, 1, wq), lambda b: (b, 0, 0)),
            pl.BlockSpec((1, 1, wq), lambda b: (b, 0, 1)),
            pl.BlockSpec((1, 1, wv), lambda b: (b, 0, 1)),
            pl.BlockSpec((1, 1, wv), lambda b: (b, 0, 2)),
            pl.BlockSpec((1, 1, 2 * heads), lambda b: (b, 0, 0)),
            pl.BlockSpec((1, 2 * heads), lambda b: (0, 0)),
            pl.BlockSpec((1, wv), lambda b: (0, 0)),
        ] + state_specs,
        out_specs=[pl.BlockSpec((1, 1, wv), lambda b: (b, 0, 0))] + state_specs,
        out_shape=[
            jax.ShapeDtypeStruct((bsz, 1, wv), F32),
            jax.ShapeDtypeStruct(c0.shape, F32),
            jax.ShapeDtypeStruct(n0.shape, F32),
            jax.ShapeDtypeStruct(n0.shape, F32),
        ],
        compiler_params=_params(("arbitrary",), 32),
        name="mlstm_step",
    )(p, p, p, p, gates, bias, headnorm, c0, n0, m0b)


def _softplus_pair(z):
    l = jnp.log(1.0 + jnp.exp(-jnp.abs(z)))
    return jnp.maximum(z, 0.0) + l, jnp.minimum(z, 0.0) - l


def _sb_seq_kernel(bias_ref, q_ref, k_ref, v_ref, o_ref, *, tq, scale):
    h = pl.program_id(1)
    qi = pl.program_id(2)
    q = q_ref[0]
    bias = bias_ref[h]
    ji = lax.broadcasted_iota(jnp.int32, (tq, tq), 0)
    si = lax.broadcasted_iota(jnp.int32, (tq, tq), 1)
    later = jnp.where(ji > si, 1.0, 0.0).astype(F32)

    def block(kb, run, mask=None, flag=None):
        off = pl.multiple_of(kb * tq, tq)
        kk = k_ref[0, pl.ds(off, tq), :]
        vv = v_ref[0, pl.ds(off, tq), :]
        z = lax.dot_general(q, kk, (((1,), (1,)), ((), ())), preferred_element_type=F32) * scale + bias
        sp, ls = _softplus_pair(z)
        if mask is not None:
            sp = jnp.where(mask, sp, 0.0)
        if flag is not None:
            sp = sp * flag
        after = jnp.dot(sp, later, preferred_element_type=F32) + run
        a = jnp.exp(ls - after)
        if mask is not None:
            a = jnp.where(mask, a, 0.0)
        if flag is not None:
            a = a * flag
        return jnp.dot(a, vv, preferred_element_type=F32), jnp.sum(sp, axis=1, keepdims=True)

    acc, run = block(qi, jnp.zeros((tq, 1), F32), mask=si < ji)

    def body(step, carry):
        acc, run = carry
        kb1 = qi - 1 - 2 * step
        kb2 = kb1 - 1
        flag = jnp.where(kb2 >= 0, 1.0, 0.0).astype(F32)
        c1, s1 = block(kb1, run)
        c2, s2 = block(jnp.maximum(kb2, 0), run + s1, flag=flag)
        return acc + (c1 + c2), run + (s1 + s2)

    acc, _ = lax.fori_loop(0, (qi + 1) // 2, body, (acc, run))
    o_ref[0] = acc


def sb_attention_seq(q, k, v, bias, *, heads, tq):
    bsz, s, width = q.shape
    hd = width // heads
    assert s % tq == 0
    return pl.pallas_call(
        functools.partial(_sb_seq_kernel, tq=tq, scale=hd ** -0.5),
        grid=(bsz, heads, s // tq),
        in_specs=[
            pl.BlockSpec(memory_space=pltpu.SMEM),
            pl.BlockSpec((1, tq, hd), lambda b, h, i: (b, i, h)),
            pl.BlockSpec((1, s, hd), lambda b, h, i: (b, 0, h)),
            pl.BlockSpec((1, s, hd), lambda b, h, i: (b, 0, h)),
        ],
        out_specs=pl.BlockSpec((1, tq, hd), lambda b, h, i: (b, i, h)),
        out_shape=jax.ShapeDtypeStruct((bsz, s, width), F32),
        compiler_params=_params(("arbitrary", "arbitrary", "arbitrary"), 32),
        name="sb_attention_seq",
    )(bias, q, k, v)


def _sb_paged_kernel(pt_ref, bias_ref, ts_ref, q_ref, *refs, heads, page, scale, pages_per_step):
    kv_refs = refs[:2 * pages_per_step]
    o_ref, acc_scr, run_scr = refs[2 * pages_per_step:]
    p = pl.program_id(1)

    @pl.when(p == 0)
    def _():
        acc_scr[...] = jnp.zeros_like(acc_scr)
        run_scr[...] = jnp.zeros_like(run_scr)

    q = q_ref[0]
    tile = 2 * LANES
    n_tiles = page * heads // tile
    hrow = lax.broadcasted_iota(jnp.int32, (heads, tile), 0)
    lane = lax.broadcasted_iota(jnp.int32, (heads, tile), 1)
    own = (lane % heads) == hrow
    trow = lax.broadcasted_iota(jnp.int32, (n_tiles, 1), 0)

    def shift_up(y, k):
        return jnp.where(trow < n_tiles - k, pltpu.roll(y, n_tiles - k, axis=0), 0.0)

    acc = acc_scr[...]
    run = run_scr[...]
    for j in range(pages_per_step):
        k_ref, v_ref = kv_refs[2 * j], kv_refs[2 * j + 1]
        r = lax.dot_general(q, k_ref[0], (((1,), (1,)), ((), ())), preferred_element_type=F32)
        z = jnp.zeros((n_tiles, tile), F32)
        for t in range(n_tiles):
            zt = jnp.sum(jnp.where(own, r[:, t * tile:(t + 1) * tile], 0.0), axis=0, keepdims=True)
            z = jnp.where(trow == t, zt, z)
        z = z * scale + bias_ref[...]
        sp, ls = _softplus_pair(z)
        st = jnp.dot(sp, ts_ref[...], preferred_element_type=F32)
        loc = st[:, :tile]
        tot = st[:, tile:]
        e = shift_up(tot, 1)
        k = 1
        while k < n_tiles:
            e = e + shift_up(e, k)
            k *= 2
        a = jnp.exp(ls - (loc + e + run))
        run = run + e[0:1] + tot[0:1]
        a_big = jnp.concatenate([jnp.where(own, a[t:t + 1, :], 0.0) for t in range(n_tiles)], axis=1)
        acc = acc + jnp.dot(a_big, v_ref[0], preferred_element_type=F32)
    acc_scr[...] = acc
    run_scr[...] = run

    @pl.when(p == pl.num_programs(1) - 1)
    def _():
        o_ref[0] = acc


def sb_attention_paged(q, cache_k, cache_v, page_table, bias):
    bsz, heads, hd = q.shape
    n_pool, page = cache_k.shape[:2]
    n_pages = page_table.shape[1]
    pps = PAGES_PER_STEP if n_pages % PAGES_PER_STEP == 0 else 1
    tile = 2 * LANES
    assert tile % heads == 0 and (page * heads) % tile == 0
    ck = cache_k.reshape(n_pool, page * heads, hd)
    cv = cache_v.reshape(n_pool, page * heads, hd)
    idx = jnp.arange(tile)
    same = (idx[:, None] % heads) == (idx[None, :] % heads)
    later = same & ((idx[:, None] // heads) > (idx[None, :] // heads))
    ts = jnp.concatenate([later, same], axis=1).astype(F32)
    bias_lanes = jnp.tile(bias, tile // heads).reshape(1, tile)

    def kv_map(j):
        return lambda b, p, pt: (pt[b, n_pages - 1 - (p * pps + j)], 0, 0)

    kv_specs = []
    for j in range(pps):
        kv_specs += [pl.BlockSpec((1, page * heads, hd), kv_map(j))] * 2
    return pl.pallas_call(
        functools.partial(_sb_paged_kernel, heads=heads, page=page, scale=hd ** -0.5, pages_per_step=pps),
        grid_spec=pltpu.PrefetchScalarGridSpec(
            num_scalar_prefetch=1,
            grid=(bsz, n_pages // pps),
            in_specs=[
                pl.BlockSpec((1, tile), lambda b, p, pt: (0, 0)),
                pl.BlockSpec((tile, 2 * tile), lambda b, p, pt: (0, 0)),
                pl.BlockSpec((1, heads, hd), lambda b, p, pt: (b, 0, 0)),
            ] + kv_specs,
            out_specs=pl.BlockSpec((1, heads, hd), lambda b, p, pt: (b, 0, 0)),
            scratch_shapes=[pltpu.VMEM((heads, hd), F32), pltpu.VMEM((1, tile), F32)],
        ),
        out_shape=jax.ShapeDtypeStruct((bsz, heads, hd), F32),
        compiler_params=_params(("arbitrary", "arbitrary"), 48),
        name="sb_attention_paged",
    )(page_table, bias_lanes, ts, q, *([ck, cv] * pps))


def _trunk(x, prm, *, seq_len, tm, mlstm_state=None, conv_state=None, paged=None):
    m, d = x.shape
    heads_a, dv = prm["headnorm_a"].shape[1:]
    dk = (prm["w_in_a"].shape[2] - 2 * heads_a * dv - 2 * heads_a) // (2 * heads_a)
    heads_b = prm["sb_bias"].shape[1]
    hd = prm["k_norm"].shape[0]
    wide = 2 * heads_a * dk + 2 * heads_a * dv
    is_seq = seq_len is not None
    bsz = m // seq_len if is_seq else m

    w_in = prm["w_in_a"][0]
    p, gates = norm_matmul(x, prm["norm_mix_a"][0], w_in, splits=(wide,), tm=tm, tn=512,
                           w_small=w_in[:, wide:])
    gate_bias = jnp.concatenate([prm["b_igate"][0], prm["b_fgate"][0]]).reshape(1, 2 * heads_a)
    headnorm = prm["headnorm_a"][0].reshape(1, heads_a * dv)
    if is_seq:
        hg, c_new, n_new, m_new = mlstm_seq(p.reshape(bsz, seq_len, wide), gates.reshape(bsz, seq_len, -1),
                                            gate_bias, headnorm, heads=heads_a, dk=dk, dv=dv)
    else:
        c0, n0, m0 = mlstm_state
        m0b = jnp.broadcast_to(m0[..., None], n0.shape)
        hg, c_new, n_new, m_new = mlstm_step(p.reshape(m, 1, wide), gates.reshape(m, 1, -1), gate_bias,
                                             headnorm, c0, n0, m0b, heads=heads_a, dk=dk, dv=dv)
    m_new = m_new[:, :, 0]
    x = matmul_residual(hg.reshape(m, heads_a * dv), prm["w_out_a"][0], x, tm=tm, tn=512)

    def ffn(x, layer):
        a = (x, prm["norm_ffn"][layer], prm["w_gate"], prm["w_up"], prm["w_down"],
             prm["conv_w"][layer], prm["conv_b"][layer])
        if is_seq:
            y, tails = conv_ffn(*a, layer=layer, tm=tm, tf=256, seq_len=seq_len)
            st = seq_len // tm
            return y, tails[st - 1::st]
        buf = conv_state[layer]
        y, u = conv_ffn(*a, layer=layer, tm=tm, tf=256, state=(buf[:, 0], buf[:, 1]))
        return y, jnp.stack([buf[:, 1], u], axis=1)

    x, conv0 = ffn(x, 0)

    width_b = heads_b * hd
    k_new, v_new = norm_matmul(x, prm["norm_kv"], prm["w_kv"], splits=(width_b, width_b), tm=tm, tn=512,
                               head_gain=prm["k_norm"], n_normed=1)

    q, = norm_matmul(x, prm["norm_mix_b"][0], prm["w_q_b"][0], splits=(width_b,), tm=tm, tn=512,
                     head_gain=prm["q_norm_b"][0], n_normed=1)
    if is_seq:
        o = sb_attention_seq(q.reshape(bsz, seq_len, -1), k_new.reshape(bsz, seq_len, -1),
                             v_new.reshape(bsz, seq_len, -1), prm["sb_bias"][0], heads=heads_b, tq=256)
        k_out, v_out = split_heads([k_new, v_new], heads=heads_b, tm=min(tm, 512))
    else:
        cache_k, cache_v, page_table = paged
        o = sb_attention_paged(q.reshape(m, heads_b, hd), cache_k, cache_v, page_table, prm["sb_bias"][0])
        k_out, v_out = k_new, v_new
    x = matmul_residual(o.reshape(m, width_b), prm["w_out_b"][0], x, tm=tm, tn=512)
    x, conv1 = ffn(x, 1)
    return (x, k_out.reshape(bsz, -1, heads_b, hd), v_out.reshape(bsz, -1, heads_b, hd),
            c_new[None], n_new[None], m_new[None], jnp.stack([conv0, conv1]))


def kernel(x_prompt, x_sample, cache_k, cache_v, page_table, state_mlstm_C, state_mlstm_n, state_mlstm_m, state_conv, norm_mix_a, w_in_a, b_igate, b_fgate, headnorm_a, w_out_a, norm_kv, w_kv, k_norm, norm_mix_b, w_q_b, q_norm_b, sb_bias, w_out_b, norm_ffn, w_gate, w_up, conv_w, conv_b, w_down):
    prm = dict(norm_mix_a=norm_mix_a, w_in_a=w_in_a, b_igate=b_igate, b_fgate=b_fgate, headnorm_a=headnorm_a,
               w_out_a=w_out_a, norm_kv=norm_kv, w_kv=w_kv, k_norm=k_norm, norm_mix_b=norm_mix_b, w_q_b=w_q_b,
               q_norm_b=q_norm_b, sb_bias=sb_bias, w_out_b=w_out_b, norm_ffn=norm_ffn, w_gate=w_gate, w_up=w_up,
               conv_w=conv_w, conv_b=conv_b, w_down=w_down)
    bp, s, d = x_prompt.shape
    yp, kp, vp, cp, np_, mp, convp = _trunk(x_prompt.reshape(bp * s, d), prm, seq_len=s, tm=min(1024, s))
    db, ds, _ = x_sample.shape
    assert ds == 1
    ys, ks, vs, cs, ns, ms, convs = _trunk(
        x_sample.reshape(db, d), prm, seq_len=None, tm=db,
        mlstm_state=(state_mlstm_C[0], state_mlstm_n[0], state_mlstm_m[0]),
        conv_state=state_conv, paged=(cache_k, cache_v, page_table))
    return (yp.reshape(bp, s, d), ys.reshape(db, 1, d), kp, vp, cp, np_, mp, convp,
            ks, vs, cs, ns, ms, convs)
```

```python
import functools

import jax
import jax.numpy as jnp
from jax import lax
from jax.experimental import pallas as pl
from jax.experimental.pallas import tpu as pltpu

F32 = jnp.float32
BF16 = jnp.bfloat16
EPS = 1e-6
MLSTM_CHUNK = 64
MIB = 1024 * 1024
LANES = 128
SUBLANES = 8
PAGES_PER_STEP = 8
FFN_SUB_ROWS = 512
FFN_TF = 512
MATMUL_SUB_ROWS = 256


def _params(semantics, vmem_mib):
    return pltpu.CompilerParams(dimension_semantics=semantics, vmem_limit_bytes=vmem_mib * MIB)


def _rms(x):
    return x * lax.rsqrt(jnp.mean(x * x, axis=-1, keepdims=True) + EPS)


def _log_sigmoid(x):
    return jnp.minimum(x, 0.0) - jnp.log1p(jnp.exp(-jnp.abs(x)))


def _norm_matmul_kernel(*refs, head_dim, n_normed_tiles, has_small, sub_rows):
    x_ref, g_ref, w_ref, hg_ref = refs[:4]
    rest = refs[4:]
    if has_small:
        ws_ref, o_ref, os_ref, h_scr = rest
    else:
        o_ref, h_scr = rest
    j = pl.program_id(1)

    @pl.when(j == 0)
    def _():
        h = _rms(x_ref[...]) * g_ref[...]
        h_scr[...] = h
        if has_small:
            os_ref[...] = jnp.dot(h, ws_ref[...], preferred_element_type=F32)

    tm, tn = o_ref.shape
    sub = min(tm, sub_rows)
    for s in range(tm // sub):
        rows = slice(s * sub, (s + 1) * sub)
        y = jnp.dot(h_scr[rows, :], w_ref[...], preferred_element_type=F32)
        if n_normed_tiles > 0:
            normed = jnp.concatenate(
                [_rms(y[:, c * head_dim:(c + 1) * head_dim]) * hg_ref[...] for c in range(tn // head_dim)],
                axis=1)
            y = jnp.where(j < n_normed_tiles, normed, y)
        o_ref[rows, :] = y


def norm_matmul(x, g, w, *, n_out, tm, tn, head_gain=None, n_normed_cols=0, w_small=None):
    m, k = x.shape
    head_dim = LANES if head_gain is None else head_gain.shape[-1]
    hg = jnp.ones((1, head_dim), F32) if head_gain is None else head_gain.reshape(1, head_dim)
    assert m % tm == 0 and tn % head_dim == 0 and n_out % tn == 0 and n_normed_cols % tn == 0
    has_small = w_small is not None
    in_specs = [
        pl.BlockSpec((tm, k), lambda i, j: (i, 0)),
        pl.BlockSpec((1, k), lambda i, j: (0, 0)),
        pl.BlockSpec((k, tn), lambda i, j: (0, j)),
        pl.BlockSpec((1, head_dim), lambda i, j: (0, 0)),
    ]
    args = [x, g.reshape(1, k), w, hg]
    out_shape = [jax.ShapeDtypeStruct((m, n_out), F32)]
    out_specs = [pl.BlockSpec((tm, tn), lambda i, j: (i, j))]
    if has_small:
        ns = w_small.shape[1]
        in_specs.append(pl.BlockSpec((k, ns), lambda i, j: (0, 0)))
        args.append(w_small)
        out_shape.append(jax.ShapeDtypeStruct((m, ns), F32))
        out_specs.append(pl.BlockSpec((tm, ns), lambda i, j: (i, 0)))
    outs = pl.pallas_call(
        functools.partial(_norm_matmul_kernel, head_dim=head_dim, n_normed_tiles=n_normed_cols // tn,
                          has_small=has_small, sub_rows=MATMUL_SUB_ROWS),
        grid=(m // tm, n_out // tn),
        in_specs=in_specs,
        out_specs=out_specs,
        out_shape=out_shape,
        scratch_shapes=[pltpu.VMEM((tm, k), F32)],
        compiler_params=_params(("arbitrary", "arbitrary"), 48),
        name="norm_matmul",
    )(*args)
    return outs if has_small else outs[0]


def _split_heads_kernel(*refs, heads, hd):
    n = len(refs) // 2
    for x_ref, o_ref in zip(refs[:n], refs[n:]):
        tm = x_ref.shape[0]
        for c in range(heads):
            o_ref[pl.ds(c, tm, stride=heads), :] = x_ref[:, c * hd:(c + 1) * hd]


def split_heads(xs, col_blocks, *, heads, hd, tm):
    m = xs[0].shape[0]
    width = heads * hd
    assert m % tm == 0
    outs = pl.pallas_call(
        functools.partial(_split_heads_kernel, heads=heads, hd=hd),
        grid=(m // tm,),
        in_specs=[pl.BlockSpec((tm, width), functools.partial(lambda c, i: (i, c), c)) for c in col_blocks],
        out_specs=[pl.BlockSpec((tm * heads, hd), lambda i: (i, 0))] * len(xs),
        out_shape=[jax.ShapeDtypeStruct((m * heads, hd), F32)] * len(xs),
        compiler_params=_params(("arbitrary",), 32),
        name="split_heads",
    )(*xs)
    return [o.reshape(m, heads, hd) for o in outs]


def _matmul_res_kernel(a_ref, w_ref, r_ref, o_ref):
    o_ref[...] = r_ref[...] + jnp.dot(a_ref[...], w_ref[...], preferred_element_type=F32)


def matmul_residual(a, w, res, *, tm, tn):
    m, k = a.shape
    n = w.shape[1]
    assert m % tm == 0 and n % tn == 0
    return pl.pallas_call(
        _matmul_res_kernel,
        grid=(m // tm, n // tn),
        in_specs=[
            pl.BlockSpec((tm, k), lambda i, j: (i, 0)),
            pl.BlockSpec((k, tn), lambda i, j: (0, j)),
            pl.BlockSpec((tm, tn), lambda i, j: (i, j)),
        ],
        out_specs=pl.BlockSpec((tm, tn), lambda i, j: (i, j)),
        out_shape=jax.ShapeDtypeStruct((m, n), F32),
        compiler_params=_params(("arbitrary", "arbitrary"), 48),
        name="matmul_residual",
    )(a, w, res)


def _ffn_kernel(*refs, seq_tiles, per_row_state, sub_rows):
    if per_row_state:
        (x_ref, g_ref, wg_ref, wu_ref, wd_ref, cw_ref, cb_ref, p2_ref, p1_ref,
         o_ref, tail_ref, h_scr) = refs
    else:
        (x_ref, g_ref, wg_ref, wu_ref, wd_ref, cw_ref, cb_ref,
         o_ref, tail_ref, h_scr, carry_scr) = refs
    i = pl.program_id(0)
    f = pl.program_id(1)

    @pl.when(f == 0)
    def _():
        x = x_ref[...]
        h_scr[...] = (_rms(x) * g_ref[...]).astype(BF16)
        o_ref[...] = x

    tm = x_ref.shape[0]
    sub = min(tm, sub_rows)
    if not per_row_state:
        @pl.when(i % seq_tiles == 0)
        def _():
            carry_scr[f] = jnp.zeros(carry_scr.shape[1:], F32)

        prev = carry_scr[f, 0:2, :]
        row = lax.broadcasted_iota(jnp.int32, (sub, 1), 0)
    for s in range(tm // sub):
        rows = slice(s * sub, (s + 1) * sub)
        h = h_scr[rows, :]
        u = jnp.dot(h, wg_ref[...].astype(BF16), preferred_element_type=F32)
        up = jnp.dot(h, wu_ref[...].astype(BF16), preferred_element_type=F32)
        if per_row_state:
            u1 = p1_ref[rows, :]
            u2 = p2_ref[rows, :]
            tail_ref[rows, :] = u
        else:
            u1 = jnp.where(row == 0, prev[1:2], pltpu.roll(u, 1, axis=0))
            u2 = jnp.where(row == 0, prev[0:1], jnp.where(row == 1, prev[1:2], pltpu.roll(u, 2, axis=0)))
            prev = u[sub - 2:sub]
        conv = cb_ref[...] + cw_ref[0:1] * u2 + cw_ref[1:2] * u1 + cw_ref[2:3] * u
        act = conv * jax.nn.sigmoid(conv) * up
        o_ref[rows, :] += jnp.dot(act.astype(BF16), wd_ref[...].astype(BF16), preferred_element_type=F32)
    if not per_row_state:
        tail_ref[0] = prev
        carry_scr[f, 0:2, :] = prev


def conv_ffn(x, g, w_gate, w_up, w_down, conv_w, conv_b, *, layer, tm, tf, seq_len=None, state=None):
    m, d = x.shape
    nf = w_gate.shape[2]
    assert m % tm == 0 and nf % tf == 0
    per_row_state = state is not None
    grid = (m // tm, nf // tf)
    row_spec = pl.BlockSpec((tm, d), lambda i, f: (i, 0), pipeline_mode=pl.Buffered(1))
    in_specs = [
        row_spec,
        pl.BlockSpec((1, d), lambda i, f: (0, 0)),
        pl.BlockSpec((None, d, tf), lambda i, f: (layer, 0, f)),
        pl.BlockSpec((None, d, tf), lambda i, f: (layer, 0, f)),
        pl.BlockSpec((None, tf, d), lambda i, f: (layer, f, 0)),
        pl.BlockSpec((3, tf), lambda i, f: (0, f)),
        pl.BlockSpec((1, tf), lambda i, f: (0, f)),
    ]
    args = [x, g.reshape(1, d), w_gate, w_up, w_down, conv_w, conv_b.reshape(1, nf)]
    scratch = [pltpu.VMEM((tm, d), BF16)]
    if per_row_state:
        in_specs += [pl.BlockSpec((tm, tf), lambda i, f: (i, f))] * 2
        args += [state[0], state[1]]
        tail_shape = jax.ShapeDtypeStruct((m, nf), F32)
        tail_spec = pl.BlockSpec((tm, tf), lambda i, f: (i, f))
        seq_tiles = 1
    else:
        assert seq_len % tm == 0
        seq_tiles = seq_len // tm
        tail_shape = jax.ShapeDtypeStruct((m // tm, 2, nf), F32)
        tail_spec = pl.BlockSpec((1, 2, tf), lambda i, f: (i, 0, f))
        scratch.append(pltpu.VMEM((nf // tf, SUBLANES, tf), F32))
    return pl.pallas_call(
        functools.partial(_ffn_kernel, seq_tiles=seq_tiles, per_row_state=per_row_state,
                          sub_rows=FFN_SUB_ROWS),
        grid=grid,
        in_specs=in_specs,
        out_specs=[row_spec, tail_spec],
        out_shape=[jax.ShapeDtypeStruct((m, d), F32), tail_shape],
        scratch_shapes=scratch,
        compiler_params=_params(("arbitrary", "arbitrary"), 56),
        name="conv_ffn_rows" if per_row_state else "conv_ffn_seq",
    )(*args)


def _mlstm_seq_kernel(q_ref, k_ref, v_ref, o_ref, gt_ref, bias_ref, hn_ref,
                      hg_ref, c_ref, n_ref, m_ref, *, heads, dk, dv, chunk):
    c = pl.program_id(1)

    @pl.when(c == 0)
    def _():
        c_ref[...] = jnp.zeros_like(c_ref)
        n_ref[...] = jnp.zeros_like(n_ref)
        m_ref[...] = jnp.zeros_like(m_ref)

    gates = gt_ref[0] + bias_ref[...]
    ig_all = gates[:, :heads]
    lf_all = _log_sigmoid(gates[:, heads:])
    ti = lax.broadcasted_iota(jnp.int32, (chunk, chunk), 0)
    si = lax.broadcasted_iota(jnp.int32, (chunk, chunk), 1)
    eye = ti == si
    lower = si <= ti
    hs = range(heads)
    qsl = [slice(h * dk, (h + 1) * dk) for h in hs]
    vsl = [slice(h * dv, (h + 1) * dv) for h in hs]
    q = [q_ref[0, :, qsl[h]] for h in hs]
    k = [k_ref[0, :, qsl[h]] * (dk ** -0.5) for h in hs]
    v = [v_ref[0, :, vsl[h]] for h in hs]
    ig_c = [ig_all[:, h:h + 1] for h in hs]
    lf_c = [lf_all[:, h:h + 1] for h in hs]
    m_prev = [m_ref[0, h:h + 1, 0:1] for h in hs]
    n_h = [n_ref[0, h:h + 1, :] for h in hs]
    lf_r = [jnp.sum(jnp.where(eye, lf_c[h], 0.0), axis=0, keepdims=True) for h in hs]
    ig_r = [jnp.sum(jnp.where(eye, ig_c[h], 0.0), axis=0, keepdims=True) for h in hs]
    b_r = [jnp.sum(jnp.where(ti <= si, lf_c[h], 0.0), axis=0, keepdims=True) for h in hs]
    b_c = [jnp.sum(jnp.where(lower, lf_r[h], 0.0), axis=1, keepdims=True) for h in hs]
    d = [jnp.where(lower, b_c[h] - b_r[h] + ig_r[h], -jnp.inf) for h in hs]
    d_max = [jnp.max(d[h], axis=1, keepdims=True) for h in hs]
    inter = [b_c[h] + m_prev[h] for h in hs]
    m_row = [jnp.maximum(inter[h], d_max[h]) for h in hs]
    qk = [lax.dot_general(q[h], k[h], (((1,), (1,)), ((), ())), preferred_element_type=F32) for h in hs]
    w = [jnp.exp(d[h] - m_row[h]) * qk[h] for h in hs]
    w_inter = [jnp.exp(inter[h] - m_row[h]) for h in hs]
    w_sum = [jnp.sum(w[h], axis=1, keepdims=True) for h in hs]
    qn = [jnp.sum(q[h] * n_h[h], axis=1, keepdims=True) for h in hs]
    num = [jnp.dot(w[h], v[h], preferred_element_type=F32)
           + w_inter[h] * jnp.dot(q[h], c_ref[0, h], preferred_element_type=F32) for h in hs]
    den = [jnp.maximum(jnp.abs(w_sum[h] + w_inter[h] * qn[h]), jnp.exp(-m_row[h])) for h in hs]
    hc = [num[h] / den[h] for h in hs]
    hc_ms = [jnp.mean(hc[h] * hc[h], axis=-1, keepdims=True) for h in hs]
    for h in hs:
        hn = hc[h] * lax.rsqrt(hc_ms[h] + EPS) * hn_ref[:, vsl[h]]
        hg_ref[0, :, vsl[h]] = hn * jax.nn.sigmoid(o_ref[0, :, vsl[h]])
    g = [b_c[h][chunk - 1:chunk, :] for h in hs]
    log_ws = [g[h] - b_c[h] + ig_c[h] for h in hs]
    m_new = [jnp.maximum(g[h] + m_prev[h], jnp.max(log_ws[h], axis=0, keepdims=True)) for h in hs]
    ws = [jnp.exp(log_ws[h] - m_new[h]) for h in hs]
    decay = [jnp.exp(g[h] + m_prev[h] - m_new[h]) for h in hs]
    for h in hs:
        c_ref[0, h] = decay[h] * c_ref[0, h] + lax.dot_general(
            k[h], ws[h] * v[h], (((0,), (0,)), ((), ())), preferred_element_type=F32)
        n_ref[0, h:h + 1, :] = decay[h] * n_h[h] + jnp.sum(ws[h] * k[h], axis=0, keepdims=True)
        m_ref[0, h:h + 1, :] = jnp.broadcast_to(m_new[h], (1, m_ref.shape[2]))


def mlstm_seq(p, gates, bias, headnorm, *, heads, dk, dv):
    bsz, s, _ = p.shape
    chunk = MLSTM_CHUNK if s % MLSTM_CHUNK == 0 else s
    nc = s // chunk
    wq = heads * dk
    wv = heads * dv
    assert wv == 2 * wq
    return pl.pallas_call(
        functools.partial(_mlstm_seq_kernel, heads=heads, dk=dk, dv=dv, chunk=chunk),
        grid=(bsz, nc),
        in_specs=[
            pl.BlockSpec((1, chunk, wq), lambda b, c: (b, c, 0)),
            pl.BlockSpec((1, chunk, wq), lambda b, c: (b, c, 1)),
            pl.BlockSpec((1, chunk, wv), lambda b, c: (b, c, 1)),
            pl.BlockSpec((1, chunk, wv), lambda b, c: (b, c, 2)),
            pl.BlockSpec((1, chunk, 2 * heads), lambda b, c: (b, c, 0)),
            pl.BlockSpec((1, 2 * heads), lambda b, c: (0, 0)),
            pl.BlockSpec((1, wv), lambda b, c: (0, 0)),
        ],
        out_specs=[
            pl.BlockSpec((1, chunk, wv), lambda b, c: (b, c, 0)),
            pl.BlockSpec((1, heads, dk, dv), lambda b, c: (b, 0, 0, 0)),
            pl.BlockSpec((1, heads, dk), lambda b, c: (b, 0, 0)),
            pl.BlockSpec((1, heads, dk), lambda b, c: (b, 0, 0)),
        ],
        out_shape=[
            jax.ShapeDtypeStruct((bsz, s, wv), F32),
            jax.ShapeDtypeStruct((bsz, heads, dk, dv), F32),
            jax.ShapeDtypeStruct((bsz, heads, dk), F32),
            jax.ShapeDtypeStruct((bsz, heads, dk), F32),
        ],
        compiler_params=_params(("arbitrary", "arbitrary"), 32),
        name="mlstm_seq",
    )(p, p, p, p, gates, bias, headnorm)


def _mlstm_step_kernel(q_ref, k_ref, v_ref, o_ref, gt_ref, bias_ref, hn_ref, c0_ref, n0_ref, m0_ref,
                       hg_ref, c_ref, n_ref, m_ref, *, heads, dk, dv):
    gates = gt_ref[0] + bias_ref[...]
    ig_all = gates[:, :heads]
    lf_all = _log_sigmoid(gates[:, heads:])
    ri = lax.broadcasted_iota(jnp.int32, (dk, dk), 0)
    ci = lax.broadcasted_iota(jnp.int32, (dk, dk), 1)
    eye = ri == ci
    for h in range(heads):
        qs = slice(h * dk, (h + 1) * dk)
        vs = slice(h * dv, (h + 1) * dv)
        q = q_ref[0, :, qs]
        k = k_ref[0, :, qs] * (dk ** -0.5)
        v = v_ref[0, :, vs]
        ig = ig_all[:, h:h + 1]
        lf = lf_all[:, h:h + 1]
        m_prev = m0_ref[0, h:h + 1, 0:1]
        c_h = c0_ref[0, h]
        n_h = n0_ref[0, h:h + 1, :]
        inter = lf + m_prev
        m_row = jnp.maximum(inter, ig)
        w = jnp.exp(ig - m_row) * jnp.sum(q * k, axis=1, keepdims=True)
        w_inter = jnp.exp(inter - m_row)
        q_rows = jnp.broadcast_to(q, (SUBLANES, dk))
        qc = jnp.dot(q_rows, c_h, preferred_element_type=F32)[0:1]
        num = w * v + w_inter * qc
        den = w + w_inter * jnp.sum(q * n_h, axis=1, keepdims=True)
        den = jnp.maximum(jnp.abs(den), jnp.exp(-m_row))
        hc = num / den
        m_new = jnp.maximum(lf + m_prev, ig)
        ws = jnp.exp(ig - m_new)
        decay = jnp.exp(lf + m_prev - m_new)
        k_col = jnp.sum(jnp.where(eye, k, 0.0), axis=1, keepdims=True)
        c_ref[0, h] = decay * c_h + k_col * (ws * v)
        n_ref[0, h:h + 1, :] = decay * n_h + ws * k
        m_ref[0, h:h + 1, :] = jnp.broadcast_to(m_new, (1, m_ref.shape[2]))
        hn = _rms(hc) * hn_ref[:, vs]
        hg_ref[0, :, vs] = hn * jax.nn.sigmoid(o_ref[0, :, vs])


def mlstm_step(p, gates, bias, headnorm, c0, n0, m0b, *, heads, dk, dv):
    bsz = p.shape[0]
    wq = heads * dk
    wv = heads * dv
    state_specs = [
        pl.BlockSpec((1, heads, dk, dv), lambda b: (b, 0, 0, 0)),
        pl.BlockSpec((1, heads, dk), lambda b: (b, 0, 0)),
        pl.BlockSpec((1, heads, dk), lambda b: (b, 0, 0)),
    ]
    return pl.pallas_call(
        functools.partial(_mlstm_step_kernel, heads=heads, dk=dk, dv=dv),
        grid=(bsz,),
        in_specs=[
            pl.BlockSpec((1, 1, wq), lambda b: (b, 0, 0)),
            pl.BlockSpec((1, 1, wq), lambda b: (b, 0, 1)),
            pl.BlockSpec((1, 1, wv), lambda b: (b, 0, 1)),
            pl.BlockSpec((1, 1, wv), lambda b: (b, 0, 2)),
            pl.BlockSpec((1, 1, 2 * heads), lambda b: (b, 0, 0)),
            pl.BlockSpec((1, 2 * heads), lambda b: (0, 0)),
            pl.BlockSpec((1, wv), lambda b: (0, 0)),
        ] + state_specs,
        out_specs=[pl.BlockSpec((1, 1, wv), lambda b: (b, 0, 0))] + state_specs,
        out_shape=[
            jax.ShapeDtypeStruct((bsz, 1, wv), F32),
            jax.ShapeDtypeStruct(c0.shape, F32),
            jax.ShapeDtypeStruct(n0.shape, F32),
            jax.ShapeDtypeStruct(n0.shape, F32),
        ],
        compiler_params=_params(("arbitrary",), 32),
        name="mlstm_step",
    )(p, p, p, p, gates, bias, headnorm, c0, n0, m0b)


def _softplus_pair(z):
    l = jnp.log(1.0 + jnp.exp(-jnp.abs(z)))
    return jnp.maximum(z, 0.0) + l, jnp.minimum(z, 0.0) - l


def _softplus(z):
    return jnp.maximum(z, 0.0) + jnp.log(1.0 + jnp.exp(-jnp.abs(z)))


def _sb_seq_kernel(bias_ref, q_ref, k_ref, v_ref, o_ref, *, tq, scale):
    h = pl.program_id(1)
    s_len = q_ref.shape[1]
    nq = s_len // tq
    bias = bias_ref[h]
    ji = lax.broadcasted_iota(jnp.int32, (tq, tq), 0)
    si = lax.broadcasted_iota(jnp.int32, (tq, tq), 1)
    later = jnp.where(ji > si, 1.0, 0.0).astype(F32)
    causal = si < ji
    qs = q_ref[0] * scale
    run = None

    def diag_masked(x):
        top = jnp.where(causal, x[:tq], 0.0)
        return top if x.shape[0] == tq else jnp.concatenate([top, x[tq:]], axis=0)

    for kb in range(nq - 1, -1, -1):
        lo = kb * tq
        kk = k_ref[0, lo:lo + tq, :]
        vv = v_ref[0, lo:lo + tq, :]
        z = lax.dot_general(qs[lo:], kk, (((1,), (1,)), ((), ())), preferred_element_type=F32) + bias
        sp = _softplus(z)
        ls = z - sp
        sp = diag_masked(sp)
        fresh = jnp.zeros((tq, 1), F32)
        run = fresh if run is None else jnp.concatenate([fresh, run], axis=0)
        after = jnp.dot(sp, later, preferred_element_type=F32) + run
        a = jnp.exp(ls - after)
        a = diag_masked(a)
        contrib = jnp.dot(a, vv, preferred_element_type=F32)
        o_ref[0, lo:lo + tq, :] = contrib[:tq]
        if kb < nq - 1:
            o_ref[0, lo + tq:, :] += contrib[tq:]
        run = run + jnp.sum(sp, axis=1, keepdims=True)


def sb_attention_seq(q, k, v, bias, *, heads, tq, k_block0=0, v_block0=0):
    bsz, s, width = q.shape
    hd = width // heads
    assert s % tq == 0
    return pl.pallas_call(
        functools.partial(_sb_seq_kernel, tq=tq, scale=hd ** -0.5),
        grid=(bsz, heads),
        in_specs=[
            pl.BlockSpec(memory_space=pltpu.SMEM),
            pl.BlockSpec((1, s, hd), lambda b, h: (b, 0, h)),
            pl.BlockSpec((1, s, hd), lambda b, h: (b, 0, k_block0 + h)),
            pl.BlockSpec((1, s, hd), lambda b, h: (b, 0, v_block0 + h)),
        ],
        out_specs=pl.BlockSpec((1, s, hd), lambda b, h: (b, 0, h)),
        out_shape=jax.ShapeDtypeStruct((bsz, s, width), F32),
        compiler_params=_params(("arbitrary", "arbitrary"), 48),
        name="sb_attention_seq",
    )(bias, q, k, v)


def _sb_paged_kernel(pt_ref, bias_ref, ts_ref, q_ref, *refs, heads, page, scale, pages_per_step):
    kv_refs = refs[:2 * pages_per_step]
    o_ref, acc_scr, run_scr = refs[2 * pages_per_step:]
    p = pl.program_id(1)

    @pl.when(p == 0)
    def _():
        acc_scr[...] = jnp.zeros_like(acc_scr)
        run_scr[...] = jnp.zeros_like(run_scr)

    q = q_ref[0]
    tile = 2 * LANES
    n_tiles = page * heads // tile
    hrow = lax.broadcasted_iota(jnp.int32, (heads, tile), 0)
    lane = lax.broadcasted_iota(jnp.int32, (heads, tile), 1)
    own = (lane % heads) == hrow
    trow = lax.broadcasted_iota(jnp.int32, (n_tiles, 1), 0)

    def shift_up(y, k):
        return jnp.where(trow < n_tiles - k, pltpu.roll(y, n_tiles - k, axis=0), 0.0)

    acc = acc_scr[...]
    run = run_scr[...]
    for j in range(pages_per_step):
        k_ref, v_ref = kv_refs[2 * j], kv_refs[2 * j + 1]
        r = lax.dot_general(q, k_ref[0], (((1,), (1,)), ((), ())), preferred_element_type=F32)
        z = jnp.zeros((n_tiles, tile), F32)
        for t in range(n_tiles):
            zt = jnp.sum(jnp.where(own, r[:, t * tile:(t + 1) * tile], 0.0), axis=0, keepdims=True)
            z = jnp.where(trow == t, zt, z)
        z = z * scale + bias_ref[...]
        sp, ls = _softplus_pair(z)
        st = jnp.dot(sp, ts_ref[...], preferred_element_type=F32)
        loc = st[:, :tile]
        tot = st[:, tile:]
        e = shift_up(tot, 1)
        k = 1
        while k < n_tiles:
            e = e + shift_up(e, k)
            k *= 2
        a = jnp.exp(ls - (loc + e + run))
        run = run + e[0:1] + tot[0:1]
        a_big = jnp.concatenate([jnp.where(own, a[t:t + 1, :], 0.0) for t in range(n_tiles)], axis=1)
        acc = acc + jnp.dot(a_big, v_ref[0], preferred_element_type=F32)
    acc_scr[...] = acc
    run_scr[...] = run

    @pl.when(p == pl.num_programs(1) - 1)
    def _():
        o_ref[0] = acc


def sb_attention_paged(q, cache_k, cache_v, page_table, bias):
    bsz, heads, hd = q.shape
    n_pool, page = cache_k.shape[:2]
    n_pages = page_table.shape[1]
    pps = PAGES_PER_STEP if n_pages % PAGES_PER_STEP == 0 else 1
    tile = 2 * LANES
    assert tile % heads == 0 and (page * heads) % tile == 0
    ck = cache_k.reshape(n_pool, page * heads, hd)
    cv = cache_v.reshape(n_pool, page * heads, hd)
    idx = jnp.arange(tile)
    same = (idx[:, None] % heads) == (idx[None, :] % heads)
    later = same & ((idx[:, None] // heads) > (idx[None, :] // heads))
    ts = jnp.concatenate([later, same], axis=1).astype(F32)
    bias_lanes = jnp.tile(bias, tile // heads).reshape(1, tile)

    def kv_map(j):
        return lambda b, p, pt: (pt[b, n_pages - 1 - (p * pps + j)], 0, 0)

    kv_specs = []
    for j in range(pps):
        kv_specs += [pl.BlockSpec((1, page * heads, hd), kv_map(j))] * 2
    return pl.pallas_call(
        functools.partial(_sb_paged_kernel, heads=heads, page=page, scale=hd ** -0.5, pages_per_step=pps),
        grid_spec=pltpu.PrefetchScalarGridSpec(
            num_scalar_prefetch=1,
            grid=(bsz, n_pages // pps),
            in_specs=[
                pl.BlockSpec((1, tile), lambda b, p, pt: (0, 0)),
                pl.BlockSpec((tile, 2 * tile), lambda b, p, pt: (0, 0)),
                pl.BlockSpec((1, heads, hd), lambda b, p, pt: (b, 0, 0)),
            ] + kv_specs,
            out_specs=pl.BlockSpec((1, heads, hd), lambda b, p, pt: (b, 0, 0)),
            scratch_shapes=[pltpu.VMEM((heads, hd), F32), pltpu.VMEM((1, tile), F32)],
        ),
        out_shape=jax.ShapeDtypeStruct((bsz, heads, hd), F32),
        compiler_params=_params(("arbitrary", "arbitrary"), 48),
        name="sb_attention_paged",
    )(page_table, bias_lanes, ts, q, *([ck, cv] * pps))


def _trunk(x, prm, *, seq_len, tm, mlstm_state=None, conv_state=None, paged=None):
    m, d = x.shape
    heads_a, dv = prm["headnorm_a"].shape[1:]
    dk = (prm["w_in_a"].shape[2] - 2 * heads_a * dv - 2 * heads_a) // (2 * heads_a)
    heads_b = prm["sb_bias"].shape[1]
    hd = prm["k_norm"].shape[0]
    wide = 2 * heads_a * dk + 2 * heads_a * dv
    is_seq = seq_len is not None
    bsz = m // seq_len if is_seq else m

    w_in = prm["w_in_a"][0]
    p, gates = norm_matmul(x, prm["norm_mix_a"][0], w_in, n_out=wide, tm=tm, tn=512,
                           w_small=w_in[:, wide:])
    gate_bias = jnp.concatenate([prm["b_igate"][0], prm["b_fgate"][0]]).reshape(1, 2 * heads_a)
    headnorm = prm["headnorm_a"][0].reshape(1, heads_a * dv)
    if is_seq:
        hg, c_new, n_new, m_new = mlstm_seq(p.reshape(bsz, seq_len, wide), gates.reshape(bsz, seq_len, -1),
                                            gate_bias, headnorm, heads=heads_a, dk=dk, dv=dv)
    else:
        c0, n0, m0 = mlstm_state
        m0b = jnp.broadcast_to(m0[..., None], n0.shape)
        hg, c_new, n_new, m_new = mlstm_step(p.reshape(m, 1, wide), gates.reshape(m, 1, -1), gate_bias,
                                             headnorm, c0, n0, m0b, heads=heads_a, dk=dk, dv=dv)
    m_new = m_new[:, :, 0]
    x = matmul_residual(hg.reshape(m, heads_a * dv), prm["w_out_a"][0], x, tm=tm, tn=512)

    def ffn(x, layer):
        a = (x, prm["norm_ffn"][layer], prm["w_gate"], prm["w_up"], prm["w_down"],
             prm["conv_w"][layer], prm["conv_b"][layer])
        if is_seq:
            y, tails = conv_ffn(*a, layer=layer, tm=tm, tf=FFN_TF, seq_len=seq_len)
            st = seq_len // tm
            return y, tails[st - 1::st]
        buf = conv_state[layer]
        y, u = conv_ffn(*a, layer=layer, tm=tm, tf=FFN_TF, state=(buf[:, 0], buf[:, 1]))
        return y, jnp.stack([buf[:, 1], u], axis=1)

    x, conv0 = ffn(x, 0)

    width_b = heads_b * hd
    kv = norm_matmul(x, prm["norm_kv"], prm["w_kv"], n_out=2 * width_b, tm=tm, tn=512,
                     head_gain=prm["k_norm"], n_normed_cols=width_b)

    q = norm_matmul(x, prm["norm_mix_b"][0], prm["w_q_b"][0], n_out=width_b, tm=tm, tn=512,
                    head_gain=prm["q_norm_b"][0], n_normed_cols=width_b)
    if is_seq:
        kv3 = kv.reshape(bsz, seq_len, 2 * width_b)
        o = sb_attention_seq(q.reshape(bsz, seq_len, -1), kv3, kv3, prm["sb_bias"][0], heads=heads_b, tq=256,
                             k_block0=0, v_block0=heads_b)
        k_out, v_out = split_heads([kv, kv], [0, 1], heads=heads_b, hd=hd, tm=min(tm, 512))
    else:
        cache_k, cache_v, page_table = paged
        o = sb_attention_paged(q.reshape(m, heads_b, hd), cache_k, cache_v, page_table, prm["sb_bias"][0])
        k_out, v_out = kv[:, :width_b], kv[:, width_b:]
    x = matmul_residual(o.reshape(m, width_b), prm["w_out_b"][0], x, tm=tm, tn=512)
    x, conv1 = ffn(x, 1)
    return (x, k_out.reshape(bsz, -1, heads_b, hd), v_out.reshape(bsz, -1, heads_b, hd),
            c_new[None], n_new[None], m_new[None], jnp.stack([conv0, conv1]))


def kernel(x_prompt, x_sample, cache_k, cache_v, page_table, state_mlstm_C, state_mlstm_n, state_mlstm_m, state_conv, norm_mix_a, w_in_a, b_igate, b_fgate, headnorm_a, w_out_a, norm_kv, w_kv, k_norm, norm_mix_b, w_q_b, q_norm_b, sb_bias, w_out_b, norm_ffn, w_gate, w_up, conv_w, conv_b, w_down):
    prm = dict(norm_mix_a=norm_mix_a, w_in_a=w_in_a, b_igate=b_igate, b_fgate=b_fgate, headnorm_a=headnorm_a,
               w_out_a=w_out_a, norm_kv=norm_kv, w_kv=w_kv, k_norm=k_norm, norm_mix_b=norm_mix_b, w_q_b=w_q_b,
               q_norm_b=q_norm_b, sb_bias=sb_bias, w_out_b=w_out_b, norm_ffn=norm_ffn, w_gate=w_gate, w_up=w_up,
               conv_w=conv_w, conv_b=conv_b, w_down=w_down)
    bp, s, d = x_prompt.shape
    yp, kp, vp, cp, np_, mp, convp = _trunk(x_prompt.reshape(bp * s, d), prm, seq_len=s, tm=min(1024, s))
    db, ds, _ = x_sample.shape
    assert ds == 1
    ys, ks, vs, cs, ns, ms, convs = _trunk(
        x_sample.reshape(db, d), prm, seq_len=None, tm=db,
        mlstm_state=(state_mlstm_C[0], state_mlstm_n[0], state_mlstm_m[0]),
        conv_state=state_conv, paged=(cache_k, cache_v, page_table))
    return (yp.reshape(bp, s, d), ys.reshape(db, 1, d), kp, vp, cp, np_, mp, convp,
            ks, vs, cs, ns, ms, convs)
```

```python
import functools

import jax
import jax.numpy as jnp
from jax import lax
from jax.experimental import pallas as pl
from jax.experimental.pallas import tpu as pltpu

F32 = jnp.float32
BF16 = jnp.bfloat16
EPS = 1e-6
MLSTM_CHUNK = 64
MIB = 1024 * 1024
LANES = 128
SUBLANES = 8
PAGES_PER_STEP = 8
FFN_SUB_ROWS = 512
FFN_TF = 512
MATMUL_SUB_ROWS = 256
PROJ_ROWS = 2048


def _params(semantics, vmem_mib):
    return pltpu.CompilerParams(dimension_semantics=semantics, vmem_limit_bytes=vmem_mib * MIB)


def _rms(x):
    return x * lax.rsqrt(jnp.mean(x * x, axis=-1, keepdims=True) + EPS)


def _log_sigmoid(x):
    return jnp.minimum(x, 0.0) - jnp.log1p(jnp.exp(-jnp.abs(x)))


def _norm_matmul_kernel(*refs, head_dim, n_normed_tiles, has_small, sub_rows, w_transposed):
    x_ref, g_ref, w_ref, hg_ref = refs[:4]
    rest = refs[4:]
    if has_small:
        ws_ref, o_ref, os_ref, h_scr = rest
    else:
        o_ref, h_scr = rest
    j = pl.program_id(1)
    dims = (((1,), (1 if w_transposed else 0,)), ((), ()))

    @pl.when(j == 0)
    def _():
        h = _rms(x_ref[...]) * g_ref[...]
        h_scr[...] = h.astype(BF16)
        if has_small:
            os_ref[...] = lax.dot_general(h, ws_ref[...], dims, preferred_element_type=F32)

    tm, tn = o_ref.shape
    sub = min(tm, sub_rows)
    for s in range(tm // sub):
        rows = slice(s * sub, (s + 1) * sub)
        y = lax.dot_general(h_scr[rows, :], w_ref[...].astype(BF16), dims, preferred_element_type=F32)
        if n_normed_tiles > 0:
            normed = jnp.concatenate(
                [_rms(y[:, c * head_dim:(c + 1) * head_dim]) * hg_ref[...] for c in range(tn // head_dim)],
                axis=1)
            y = jnp.where(j < n_normed_tiles, normed, y)
        o_ref[rows, :] = y


def norm_matmul(x, g, w, *, n_out, tm, tn, head_gain=None, n_normed_cols=0, w_small=None, w_transposed=False):
    m, k = x.shape
    head_dim = LANES if head_gain is None else head_gain.shape[-1]
    hg = jnp.ones((1, head_dim), F32) if head_gain is None else head_gain.reshape(1, head_dim)
    assert m % tm == 0 and tn % head_dim == 0 and n_out % tn == 0 and n_normed_cols % tn == 0
    has_small = w_small is not None
    in_specs = [
        pl.BlockSpec((tm, k), lambda i, j: (i, 0), pipeline_mode=pl.Buffered(1)),
        pl.BlockSpec((1, k), lambda i, j: (0, 0)),
        pl.BlockSpec((tn, k), lambda i, j: (j, 0)) if w_transposed else pl.BlockSpec((k, tn), lambda i, j: (0, j)),
        pl.BlockSpec((1, head_dim), lambda i, j: (0, 0)),
    ]
    args = [x, g.reshape(1, k), w, hg]
    out_shape = [jax.ShapeDtypeStruct((m, n_out), F32)]
    out_specs = [pl.BlockSpec((tm, tn), lambda i, j: (i, j))]
    if has_small:
        ns = w_small.shape[0 if w_transposed else 1]
        in_specs.append(pl.BlockSpec(w_small.shape, lambda i, j: (0, 0)))
        args.append(w_small)
        out_shape.append(jax.ShapeDtypeStruct((m, ns), F32))
        out_specs.append(pl.BlockSpec((tm, ns), lambda i, j: (i, 0)))
    outs = pl.pallas_call(
        functools.partial(_norm_matmul_kernel, head_dim=head_dim, n_normed_tiles=n_normed_cols // tn,
                          has_small=has_small, sub_rows=MATMUL_SUB_ROWS if n_normed_cols else tm,
                          w_transposed=w_transposed),
        grid=(m // tm, n_out // tn),
        in_specs=in_specs,
        out_specs=out_specs,
        out_shape=out_shape,
        scratch_shapes=[pltpu.VMEM((tm, k), BF16)],
        compiler_params=_params(("arbitrary", "arbitrary"), 56),
        name="norm_matmul",
    )(*args)
    return outs if has_small else outs[0]


def _split_heads_kernel(*refs, heads, hd):
    n = len(refs) // 2
    for x_ref, o_ref in zip(refs[:n], refs[n:]):
        tm = x_ref.shape[0]
        for c in range(heads):
            o_ref[pl.ds(c, tm, stride=heads), :] = x_ref[:, c * hd:(c + 1) * hd]


def split_heads(xs, col_blocks, *, heads, hd, tm):
    m = xs[0].shape[0]
    width = heads * hd
    assert m % tm == 0
    outs = pl.pallas_call(
        functools.partial(_split_heads_kernel, heads=heads, hd=hd),
        grid=(m // tm,),
        in_specs=[pl.BlockSpec((tm, width), functools.partial(lambda c, i: (i, c), c)) for c in col_blocks],
        out_specs=[pl.BlockSpec((tm * heads, hd), lambda i: (i, 0))] * len(xs),
        out_shape=[jax.ShapeDtypeStruct((m * heads, hd), F32)] * len(xs),
        compiler_params=_params(("arbitrary",), 32),
        name="split_heads",
    )(*xs)
    return [o.reshape(m, heads, hd) for o in outs]


def _matmul_res_kernel(a_ref, w_ref, r_ref, o_ref):
    o_ref[...] = r_ref[...] + jnp.dot(a_ref[...], w_ref[...], preferred_element_type=F32)


def matmul_residual(a, w, res, *, tm, tn):
    m, k = a.shape
    n = w.shape[1]
    assert m % tm == 0 and n % tn == 0
    return pl.pallas_call(
        _matmul_res_kernel,
        grid=(m // tm, n // tn),
        in_specs=[
            pl.BlockSpec((tm, k), lambda i, j: (i, 0), pipeline_mode=pl.Buffered(1)),
            pl.BlockSpec((k, tn), lambda i, j: (0, j)),
            pl.BlockSpec((tm, tn), lambda i, j: (i, j)),
        ],
        out_specs=pl.BlockSpec((tm, tn), lambda i, j: (i, j)),
        out_shape=jax.ShapeDtypeStruct((m, n), F32),
        compiler_params=_params(("arbitrary", "arbitrary"), 48),
        name="matmul_residual",
    )(a, w, res)


def _ffn_kernel(*refs, seq_tiles, per_row_state, sub_rows):
    if per_row_state:
        (x_ref, g_ref, wg_ref, wu_ref, wd_ref, cw_ref, cb_ref, p2_ref, p1_ref,
         o_ref, tail_ref, h_scr) = refs
    else:
        (x_ref, g_ref, wg_ref, wu_ref, wd_ref, cw_ref, cb_ref,
         o_ref, tail_ref, h_scr, carry_scr) = refs
    i = pl.program_id(0)
    f = pl.program_id(1)

    @pl.when(f == 0)
    def _():
        x = x_ref[...]
        h_scr[...] = (_rms(x) * g_ref[...]).astype(BF16)
        o_ref[...] = x

    tm = x_ref.shape[0]
    sub = min(tm, sub_rows)
    if not per_row_state:
        @pl.when(i % seq_tiles == 0)
        def _():
            carry_scr[f] = jnp.zeros(carry_scr.shape[1:], F32)

        prev = carry_scr[f, 0:2, :]
        row = lax.broadcasted_iota(jnp.int32, (sub, 1), 0)
    for s in range(tm // sub):
        rows = slice(s * sub, (s + 1) * sub)
        h = h_scr[rows, :]
        u = jnp.dot(h, wg_ref[...].astype(BF16), preferred_element_type=F32)
        up = jnp.dot(h, wu_ref[...].astype(BF16), preferred_element_type=F32)
        if per_row_state:
            u1 = p1_ref[rows, :]
            u2 = p2_ref[rows, :]
            tail_ref[rows, :] = u
        else:
            u1 = jnp.where(row == 0, prev[1:2], pltpu.roll(u, 1, axis=0))
            u2 = jnp.where(row == 0, prev[0:1], jnp.where(row == 1, prev[1:2], pltpu.roll(u, 2, axis=0)))
            prev = u[sub - 2:sub]
        conv = cb_ref[...] + cw_ref[0:1] * u2 + cw_ref[1:2] * u1 + cw_ref[2:3] * u
        act = conv * jax.nn.sigmoid(conv) * up
        o_ref[rows, :] += jnp.dot(act.astype(BF16), wd_ref[...].astype(BF16), preferred_element_type=F32)
    if not per_row_state:
        tail_ref[0] = prev
        carry_scr[f, 0:2, :] = prev


def conv_ffn(x, g, w_gate, w_up, w_down, conv_w, conv_b, *, layer, tm, tf, seq_len=None, state=None):
    m, d = x.shape
    nf = w_gate.shape[2]
    assert m % tm == 0 and nf % tf == 0
    per_row_state = state is not None
    grid = (m // tm, nf // tf)
    row_spec = pl.BlockSpec((tm, d), lambda i, f: (i, 0), pipeline_mode=pl.Buffered(1))
    in_specs = [
        row_spec,
        pl.BlockSpec((1, d), lambda i, f: (0, 0)),
        pl.BlockSpec((None, d, tf), lambda i, f: (layer, 0, f)),
        pl.BlockSpec((None, d, tf), lambda i, f: (layer, 0, f)),
        pl.BlockSpec((None, tf, d), lambda i, f: (layer, f, 0)),
        pl.BlockSpec((3, tf), lambda i, f: (0, f)),
        pl.BlockSpec((1, tf), lambda i, f: (0, f)),
    ]
    args = [x, g.reshape(1, d), w_gate, w_up, w_down, conv_w, conv_b.reshape(1, nf)]
    scratch = [pltpu.VMEM((tm, d), BF16)]
    if per_row_state:
        in_specs += [pl.BlockSpec((tm, tf), lambda i, f: (i, f))] * 2
        args += [state[0], state[1]]
        tail_shape = jax.ShapeDtypeStruct((m, nf), F32)
        tail_spec = pl.BlockSpec((tm, tf), lambda i, f: (i, f))
        seq_tiles = 1
    else:
        assert seq_len % tm == 0
        seq_tiles = seq_len // tm
        tail_shape = jax.ShapeDtypeStruct((m // tm, 2, nf), F32)
        tail_spec = pl.BlockSpec((1, 2, tf), lambda i, f: (i, 0, f))
        scratch.append(pltpu.VMEM((nf // tf, SUBLANES, tf), F32))
    return pl.pallas_call(
        functools.partial(_ffn_kernel, seq_tiles=seq_tiles, per_row_state=per_row_state,
                          sub_rows=FFN_SUB_ROWS),
        grid=grid,
        in_specs=in_specs,
        out_specs=[row_spec, tail_spec],
        out_shape=[jax.ShapeDtypeStruct((m, d), F32), tail_shape],
        scratch_shapes=scratch,
        compiler_params=_params(("arbitrary", "arbitrary"), 56),
        name="conv_ffn_rows" if per_row_state else "conv_ffn_seq",
    )(*args)


def _mlstm_seq_kernel(q_ref, k_ref, v_ref, o_ref, gt_ref, bias_ref, hn_ref,
                      hg_ref, c_ref, n_ref, m_ref, *, heads, dk, dv, chunk):
    c = pl.program_id(1)

    @pl.when(c == 0)
    def _():
        c_ref[...] = jnp.zeros_like(c_ref)
        n_ref[...] = jnp.zeros_like(n_ref)
        m_ref[...] = jnp.zeros_like(m_ref)

    gates = gt_ref[0] + bias_ref[...]
    ig_all = gates[:, :heads]
    lf_all = _log_sigmoid(gates[:, heads:])
    ti = lax.broadcasted_iota(jnp.int32, (chunk, chunk), 0)
    si = lax.broadcasted_iota(jnp.int32, (chunk, chunk), 1)
    eye = ti == si
    lower = si <= ti
    hs = range(heads)
    qsl = [slice(h * dk, (h + 1) * dk) for h in hs]
    vsl = [slice(h * dv, (h + 1) * dv) for h in hs]
    q = [q_ref[0, :, qsl[h]] for h in hs]
    k = [k_ref[0, :, qsl[h]] * (dk ** -0.5) for h in hs]
    v = [v_ref[0, :, vsl[h]] for h in hs]
    ig_c = [ig_all[:, h:h + 1] for h in hs]
    lf_c = [lf_all[:, h:h + 1] for h in hs]
    m_prev = [m_ref[0, h:h + 1, 0:1] for h in hs]
    n_h = [n_ref[0, h:h + 1, :] for h in hs]
    lf_r = [jnp.sum(jnp.where(eye, lf_c[h], 0.0), axis=0, keepdims=True) for h in hs]
    ig_r = [jnp.sum(jnp.where(eye, ig_c[h], 0.0), axis=0, keepdims=True) for h in hs]
    b_r = [jnp.sum(jnp.where(ti <= si, lf_c[h], 0.0), axis=0, keepdims=True) for h in hs]
    b_c = [jnp.sum(jnp.where(lower, lf_r[h], 0.0), axis=1, keepdims=True) for h in hs]
    d = [jnp.where(lower, b_c[h] - b_r[h] + ig_r[h], -jnp.inf) for h in hs]
    d_max = [jnp.max(d[h], axis=1, keepdims=True) for h in hs]
    inter = [b_c[h] + m_prev[h] for h in hs]
    m_row = [jnp.maximum(inter[h], d_max[h]) for h in hs]
    qk = [lax.dot_general(q[h], k[h], (((1,), (1,)), ((), ())), preferred_element_type=F32) for h in hs]
    w = [jnp.exp(d[h] - m_row[h]) * qk[h] for h in hs]
    w_inter = [jnp.exp(inter[h] - m_row[h]) for h in hs]
    w_sum = [jnp.sum(w[h], axis=1, keepdims=True) for h in hs]
    qn = [jnp.sum(q[h] * n_h[h], axis=1, keepdims=True) for h in hs]
    num = [jnp.dot(w[h], v[h], preferred_element_type=F32)
           + w_inter[h] * jnp.dot(q[h], c_ref[0, h], preferred_element_type=F32) for h in hs]
    den = [jnp.maximum(jnp.abs(w_sum[h] + w_inter[h] * qn[h]), jnp.exp(-m_row[h])) for h in hs]
    hc = [num[h] / den[h] for h in hs]
    hc_ms = [jnp.mean(hc[h] * hc[h], axis=-1, keepdims=True) for h in hs]
    for h in hs:
        hn = hc[h] * lax.rsqrt(hc_ms[h] + EPS) * hn_ref[:, vsl[h]]
        hg_ref[0, :, vsl[h]] = hn * jax.nn.sigmoid(o_ref[0, :, vsl[h]])
    g = [b_c[h][chunk - 1:chunk, :] for h in hs]
    log_ws = [g[h] - b_c[h] + ig_c[h] for h in hs]
    m_new = [jnp.maximum(g[h] + m_prev[h], jnp.max(log_ws[h], axis=0, keepdims=True)) for h in hs]
    ws = [jnp.exp(log_ws[h] - m_new[h]) for h in hs]
    decay = [jnp.exp(g[h] + m_prev[h] - m_new[h]) for h in hs]
    for h in hs:
        c_ref[0, h] = decay[h] * c_ref[0, h] + lax.dot_general(
            k[h], ws[h] * v[h], (((0,), (0,)), ((), ())), preferred_element_type=F32)
        n_ref[0, h:h + 1, :] = decay[h] * n_h[h] + jnp.sum(ws[h] * k[h], axis=0, keepdims=True)
        m_ref[0, h:h + 1, :] = jnp.broadcast_to(m_new[h], (1, m_ref.shape[2]))


def mlstm_seq(p, gates, bias, headnorm, *, heads, dk, dv):
    bsz, s, _ = p.shape
    chunk = MLSTM_CHUNK if s % MLSTM_CHUNK == 0 else s
    nc = s // chunk
    wq = heads * dk
    wv = heads * dv
    assert wv == 2 * wq
    return pl.pallas_call(
        functools.partial(_mlstm_seq_kernel, heads=heads, dk=dk, dv=dv, chunk=chunk),
        grid=(bsz, nc),
        in_specs=[
            pl.BlockSpec((1, chunk, wq), lambda b, c: (b, c, 0)),
            pl.BlockSpec((1, chunk, wq), lambda b, c: (b, c, 1)),
            pl.BlockSpec((1, chunk, wv), lambda b, c: (b, c, 1)),
            pl.BlockSpec((1, chunk, wv), lambda b, c: (b, c, 2)),
            pl.BlockSpec((1, chunk, 2 * heads), lambda b, c: (b, c, 0)),
            pl.BlockSpec((1, 2 * heads), lambda b, c: (0, 0)),
            pl.BlockSpec((1, wv), lambda b, c: (0, 0)),
        ],
        out_specs=[
            pl.BlockSpec((1, chunk, wv), lambda b, c: (b, c, 0)),
            pl.BlockSpec((1, heads, dk, dv), lambda b, c: (b, 0, 0, 0)),
            pl.BlockSpec((1, heads, dk), lambda b, c: (b, 0, 0)),
            pl.BlockSpec((1, heads, dk), lambda b, c: (b, 0, 0)),
        ],
        out_shape=[
            jax.ShapeDtypeStruct((bsz, s, wv), F32),
            jax.ShapeDtypeStruct((bsz, heads, dk, dv), F32),
            jax.ShapeDtypeStruct((bsz, heads, dk), F32),
            jax.ShapeDtypeStruct((bsz, heads, dk), F32),
        ],
        compiler_params=_params(("arbitrary", "arbitrary"), 32),
        name="mlstm_seq",
    )(p, p, p, p, gates, bias, headnorm)


def _mlstm_step_kernel(q_ref, k_ref, v_ref, o_ref, gt_ref, bias_ref, hn_ref, c0_ref, n0_ref, m0_ref,
                       hg_ref, c_ref, n_ref, m_ref, *, heads, dk, dv):
    gates = gt_ref[0] + bias_ref[...]
    ig_all = gates[:, :heads]
    lf_all = _log_sigmoid(gates[:, heads:])
    ri = lax.broadcasted_iota(jnp.int32, (dk, dk), 0)
    ci = lax.broadcasted_iota(jnp.int32, (dk, dk), 1)
    eye = ri == ci
    hs = range(heads)
    qsl = [slice(h * dk, (h + 1) * dk) for h in hs]
    vsl = [slice(h * dv, (h + 1) * dv) for h in hs]
    q = [q_ref[0, :, qsl[h]] for h in hs]
    k = [k_ref[0, :, qsl[h]] * (dk ** -0.5) for h in hs]
    v = [v_ref[0, :, vsl[h]] for h in hs]
    ig = [ig_all[:, h:h + 1] for h in hs]
    lf = [lf_all[:, h:h + 1] for h in hs]
    m_prev = [m0_ref[0, h:h + 1, 0:1] for h in hs]
    n_h = [n0_ref[0, h:h + 1, :] for h in hs]
    inter = [lf[h] + m_prev[h] for h in hs]
    m_row = [jnp.maximum(inter[h], ig[h]) for h in hs]
    qk = [jnp.sum(q[h] * k[h], axis=1, keepdims=True) for h in hs]
    qn = [jnp.sum(q[h] * n_h[h], axis=1, keepdims=True) for h in hs]
    k_col = [jnp.sum(jnp.where(eye, k[h], 0.0), axis=1, keepdims=True) for h in hs]
    w = [jnp.exp(ig[h] - m_row[h]) * qk[h] for h in hs]
    w_inter = [jnp.exp(inter[h] - m_row[h]) for h in hs]
    qc = [jnp.dot(jnp.broadcast_to(q[h], (SUBLANES, dk)), c0_ref[0, h], preferred_element_type=F32)[0:1]
          for h in hs]
    num = [w[h] * v[h] + w_inter[h] * qc[h] for h in hs]
    den = [jnp.maximum(jnp.abs(w[h] + w_inter[h] * qn[h]), jnp.exp(-m_row[h])) for h in hs]
    hc = [num[h] / den[h] for h in hs]
    hc_ms = [jnp.mean(hc[h] * hc[h], axis=-1, keepdims=True) for h in hs]
    m_new = [jnp.maximum(lf[h] + m_prev[h], ig[h]) for h in hs]
    ws = [jnp.exp(ig[h] - m_new[h]) for h in hs]
    decay = [jnp.exp(lf[h] + m_prev[h] - m_new[h]) for h in hs]
    for h in hs:
        hn = hc[h] * lax.rsqrt(hc_ms[h] + EPS) * hn_ref[:, vsl[h]]
        hg_ref[0, :, vsl[h]] = hn * jax.nn.sigmoid(o_ref[0, :, vsl[h]])
        c_ref[0, h] = decay[h] * c0_ref[0, h] + k_col[h] * (ws[h] * v[h])
        n_ref[0, h:h + 1, :] = decay[h] * n_h[h] + ws[h] * k[h]
        m_ref[0, h:h + 1, :] = jnp.broadcast_to(m_new[h], (1, m_ref.shape[2]))


def mlstm_step(p, gates, bias, headnorm, c0, n0, m0b, *, heads, dk, dv):
    bsz = p.shape[0]
    wq = heads * dk
    wv = heads * dv
    state_specs = [
        pl.BlockSpec((1, heads, dk, dv), lambda b: (b, 0, 0, 0)),
        pl.BlockSpec((1, heads, dk), lambda b: (b, 0, 0)),
        pl.BlockSpec((1, heads, dk), lambda b: (b, 0, 0)),
    ]
    return pl.pallas_call(
        functools.partial(_mlstm_step_kernel, heads=heads, dk=dk, dv=dv),
        grid=(bsz,),
        in_specs=[
            pl.BlockSpec((1, 1, wq), lambda b: (b, 0, 0)),
            pl.BlockSpec((1, 1, wq), lambda b: (b, 0, 1)),
            pl.BlockSpec((1, 1, wv), lambda b: (b, 0, 1)),
            pl.BlockSpec((1, 1, wv), lambda b: (b, 0, 2)),
            pl.BlockSpec((1, 1, 2 * heads), lambda b: (b, 0, 0)),
            pl.BlockSpec((1, 2 * heads), lambda b: (0, 0)),
            pl.BlockSpec((1, wv), lambda b: (0, 0)),
        ] + state_specs,
        out_specs=[pl.BlockSpec((1, 1, wv), lambda b: (b, 0, 0))] + state_specs,
        out_shape=[
            jax.ShapeDtypeStruct((bsz, 1, wv), F32),
            jax.ShapeDtypeStruct(c0.shape, F32),
            jax.ShapeDtypeStruct(n0.shape, F32),
            jax.ShapeDtypeStruct(n0.shape, F32),
        ],
        compiler_params=_params(("arbitrary",), 32),
        name="mlstm_step",
    )(p, p, p, p, gates, bias, headnorm, c0, n0, m0b)


def _softplus_pair(z):
    l = jnp.log(1.0 + jnp.exp(-jnp.abs(z)))
    return jnp.maximum(z, 0.0) + l, jnp.minimum(z, 0.0) - l


def _softplus(z):
    return jnp.maximum(z, 0.0) + jnp.log(1.0 + jnp.exp(-jnp.abs(z)))


def _sb_seq_kernel(bias_ref, q_ref, k_ref, v_ref, o_ref, *, tq, scale):
    h = pl.program_id(1)
    s_len = q_ref.shape[1]
    nq = s_len // tq
    bias = bias_ref[h]
    ji = lax.broadcasted_iota(jnp.int32, (tq, tq), 0)
    si = lax.broadcasted_iota(jnp.int32, (tq, tq), 1)
    later = jnp.where(ji > si, 1.0, 0.0).astype(F32)
    causal = si < ji
    qs = q_ref[0] * scale
    run = None

    def diag_masked(x):
        top = jnp.where(causal, x[:tq], 0.0)
        return top if x.shape[0] == tq else jnp.concatenate([top, x[tq:]], axis=0)

    for kb in range(nq - 1, -1, -1):
        lo = kb * tq
        kk = k_ref[0, lo:lo + tq, :]
        vv = v_ref[0, lo:lo + tq, :]
        z = lax.dot_general(qs[lo:], kk, (((1,), (1,)), ((), ())), preferred_element_type=F32) + bias
        sp = _softplus(z)
        ls = z - sp
        sp = diag_masked(sp)
        fresh = jnp.zeros((tq, 1), F32)
        run = fresh if run is None else jnp.concatenate([fresh, run], axis=0)
        after = jnp.dot(sp, later, preferred_element_type=F32) + run
        a = jnp.exp(ls - after)
        a = diag_masked(a)
        contrib = jnp.dot(a, vv, preferred_element_type=F32)
        o_ref[0, lo:lo + tq, :] = contrib[:tq]
        if kb < nq - 1:
            o_ref[0, lo + tq:, :] += contrib[tq:]
        run = run + jnp.sum(sp, axis=1, keepdims=True)


def sb_attention_seq(q, k, v, bias, *, heads, tq, k_block0=0, v_block0=0):
    bsz, s, width = q.shape
    hd = width // heads
    assert s % tq == 0
    return pl.pallas_call(
        functools.partial(_sb_seq_kernel, tq=tq, scale=hd ** -0.5),
        grid=(bsz, heads),
        in_specs=[
            pl.BlockSpec(memory_space=pltpu.SMEM),
            pl.BlockSpec((1, s, hd), lambda b, h: (b, 0, h)),
            pl.BlockSpec((1, s, hd), lambda b, h: (b, 0, k_block0 + h)),
            pl.BlockSpec((1, s, hd), lambda b, h: (b, 0, v_block0 + h)),
        ],
        out_specs=pl.BlockSpec((1, s, hd), lambda b, h: (b, 0, h)),
        out_shape=jax.ShapeDtypeStruct((bsz, s, width), F32),
        compiler_params=_params(("arbitrary", "arbitrary"), 48),
        name="sb_attention_seq",
    )(bias, q, k, v)


def _sb_paged_kernel(pt_ref, bias_ref, ts_ref, q_ref, *refs, heads, page, scale, pages_per_step):
    kv_refs = refs[:2 * pages_per_step]
    o_ref, acc_scr, run_scr = refs[2 * pages_per_step:]
    p = pl.program_id(1)

    @pl.when(p == 0)
    def _():
        acc_scr[...] = jnp.zeros_like(acc_scr)
        run_scr[...] = jnp.zeros_like(run_scr)

    q = q_ref[0]
    tile = 2 * LANES
    n_tiles = page * heads // tile
    hrow = lax.broadcasted_iota(jnp.int32, (heads, tile), 0)
    lane = lax.broadcasted_iota(jnp.int32, (heads, tile), 1)
    own = (lane % heads) == hrow
    trow = lax.broadcasted_iota(jnp.int32, (n_tiles, 1), 0)

    def shift_up(y, k):
        return jnp.where(trow < n_tiles - k, pltpu.roll(y, n_tiles - k, axis=0), 0.0)

    acc = acc_scr[...]
    run = run_scr[...]
    for j in range(pages_per_step):
        k_ref, v_ref = kv_refs[2 * j], kv_refs[2 * j + 1]
        r = lax.dot_general(q, k_ref[0], (((1,), (1,)), ((), ())), preferred_element_type=F32)
        z = jnp.zeros((n_tiles, tile), F32)
        for t in range(n_tiles):
            zt = jnp.sum(jnp.where(own, r[:, t * tile:(t + 1) * tile], 0.0), axis=0, keepdims=True)
            z = jnp.where(trow == t, zt, z)
        z = z * scale + bias_ref[...]
        sp, ls = _softplus_pair(z)
        st = jnp.dot(sp, ts_ref[...], preferred_element_type=F32)
        loc = st[:, :tile]
        tot = st[:, tile:]
        e = shift_up(tot, 1)
        k = 1
        while k < n_tiles:
            e = e + shift_up(e, k)
            k *= 2
        a = jnp.exp(ls - (loc + e + run))
        run = run + e[0:1] + tot[0:1]
        a_big = jnp.concatenate([jnp.where(own, a[t:t + 1, :], 0.0) for t in range(n_tiles)], axis=1)
        acc = acc + jnp.dot(a_big, v_ref[0], preferred_element_type=F32)
    acc_scr[...] = acc
    run_scr[...] = run

    @pl.when(p == pl.num_programs(1) - 1)
    def _():
        o_ref[0] = acc


def sb_attention_paged(q, cache_k, cache_v, page_table, bias):
    bsz, heads, hd = q.shape
    n_pool, page = cache_k.shape[:2]
    n_pages = page_table.shape[1]
    pps = PAGES_PER_STEP if n_pages % PAGES_PER_STEP == 0 else 1
    tile = 2 * LANES
    assert tile % heads == 0 and (page * heads) % tile == 0
    ck = cache_k.reshape(n_pool, page * heads, hd)
    cv = cache_v.reshape(n_pool, page * heads, hd)
    idx = jnp.arange(tile)
    same = (idx[:, None] % heads) == (idx[None, :] % heads)
    later = same & ((idx[:, None] // heads) > (idx[None, :] // heads))
    ts = jnp.concatenate([later, same], axis=1).astype(F32)
    bias_lanes = jnp.tile(bias, tile // heads).reshape(1, tile)

    def kv_map(j):
        return lambda b, p, pt: (pt[b, n_pages - 1 - (p * pps + j)], 0, 0)

    kv_specs = []
    for j in range(pps):
        kv_specs += [pl.BlockSpec((1, page * heads, hd), kv_map(j))] * 2
    return pl.pallas_call(
        functools.partial(_sb_paged_kernel, heads=heads, page=page, scale=hd ** -0.5, pages_per_step=pps),
        grid_spec=pltpu.PrefetchScalarGridSpec(
            num_scalar_prefetch=1,
            grid=(bsz, n_pages // pps),
            in_specs=[
                pl.BlockSpec((1, tile), lambda b, p, pt: (0, 0)),
                pl.BlockSpec((tile, 2 * tile), lambda b, p, pt: (0, 0)),
                pl.BlockSpec((1, heads, hd), lambda b, p, pt: (b, 0, 0)),
            ] + kv_specs,
            out_specs=pl.BlockSpec((1, heads, hd), lambda b, p, pt: (b, 0, 0)),
            scratch_shapes=[pltpu.VMEM((heads, hd), F32), pltpu.VMEM((1, tile), F32)],
        ),
        out_shape=jax.ShapeDtypeStruct((bsz, heads, hd), F32),
        compiler_params=_params(("arbitrary", "arbitrary"), 48),
        name="sb_attention_paged",
    )(page_table, bias_lanes, ts, q, *([ck, cv] * pps))


def _trunk(x, prm, *, seq_len, tm, mlstm_state=None, conv_state=None, paged=None):
    m, d = x.shape
    heads_a, dv = prm["headnorm_a"].shape[1:]
    dk = (prm["w_in_a"].shape[2] - 2 * heads_a * dv - 2 * heads_a) // (2 * heads_a)
    heads_b = prm["sb_bias"].shape[1]
    hd = prm["k_norm"].shape[0]
    wide = 2 * heads_a * dk + 2 * heads_a * dv
    is_seq = seq_len is not None
    bsz = m // seq_len if is_seq else m
    tp = min(m, PROJ_ROWS)

    w_in_t = jnp.swapaxes(prm["w_in_a"][0], 0, 1)
    p, gates = norm_matmul(x, prm["norm_mix_a"][0], w_in_t, n_out=wide, tm=tp, tn=512,
                           w_small=w_in_t[wide:], w_transposed=True)
    gate_bias = jnp.concatenate([prm["b_igate"][0], prm["b_fgate"][0]]).reshape(1, 2 * heads_a)
    headnorm = prm["headnorm_a"][0].reshape(1, heads_a * dv)
    if is_seq:
        hg, c_new, n_new, m_new = mlstm_seq(p.reshape(bsz, seq_len, wide), gates.reshape(bsz, seq_len, -1),
                                            gate_bias, headnorm, heads=heads_a, dk=dk, dv=dv)
    else:
        c0, n0, m0 = mlstm_state
        m0b = jnp.broadcast_to(m0[..., None], n0.shape)
        hg, c_new, n_new, m_new = mlstm_step(p.reshape(m, 1, wide), gates.reshape(m, 1, -1), gate_bias,
                                             headnorm, c0, n0, m0b, heads=heads_a, dk=dk, dv=dv)
    m_new = m_new[:, :, 0]
    x = matmul_residual(hg.reshape(m, heads_a * dv), prm["w_out_a"][0], x, tm=tp, tn=512)

    def ffn(x, layer):
        a = (x, prm["norm_ffn"][layer], prm["w_gate"], prm["w_up"], prm["w_down"],
             prm["conv_w"][layer], prm["conv_b"][layer])
        if is_seq:
            y, tails = conv_ffn(*a, layer=layer, tm=tm, tf=FFN_TF, seq_len=seq_len)
            st = seq_len // tm
            return y, tails[st - 1::st]
        buf = conv_state[layer]
        y, u = conv_ffn(*a, layer=layer, tm=tm, tf=FFN_TF, state=(buf[:, 0], buf[:, 1]))
        return y, jnp.stack([buf[:, 1], u], axis=1)

    x, conv0 = ffn(x, 0)

    width_b = heads_b * hd
    kv = norm_matmul(x, prm["norm_kv"], prm["w_kv"], n_out=2 * width_b, tm=tp, tn=512,
                     head_gain=prm["k_norm"], n_normed_cols=width_b)

    q = norm_matmul(x, prm["norm_mix_b"][0], prm["w_q_b"][0], n_out=width_b, tm=tp, tn=512,
                    head_gain=prm["q_norm_b"][0], n_normed_cols=width_b)
    if is_seq:
        kv3 = kv.reshape(bsz, seq_len, 2 * width_b)
        o = sb_attention_seq(q.reshape(bsz, seq_len, -1), kv3, kv3, prm["sb_bias"][0], heads=heads_b, tq=256,
                             k_block0=0, v_block0=heads_b)
        k_out, v_out = split_heads([kv, kv], [0, 1], heads=heads_b, hd=hd, tm=min(tm, 512))
    else:
        cache_k, cache_v, page_table = paged
        o = sb_attention_paged(q.reshape(m, heads_b, hd), cache_k, cache_v, page_table, prm["sb_bias"][0])
        k_out, v_out = kv[:, :width_b], kv[:, width_b:]
    x = matmul_residual(o.reshape(m, width_b), prm["w_out_b"][0], x, tm=tp, tn=512)
    x, conv1 = ffn(x, 1)
    return (x, k_out.reshape(bsz, -1, heads_b, hd), v_out.reshape(bsz, -1, heads_b, hd),
            c_new[None], n_new[None], m_new[None], jnp.stack([conv0, conv1]))


def kernel(x_prompt, x_sample, cache_k, cache_v, page_table, state_mlstm_C, state_mlstm_n, state_mlstm_m, state_conv, norm_mix_a, w_in_a, b_igate, b_fgate, headnorm_a, w_out_a, norm_kv, w_kv, k_norm, norm_mix_b, w_q_b, q_norm_b, sb_bias, w_out_b, norm_ffn, w_gate, w_up, conv_w, conv_b, w_down):
    prm = dict(norm_mix_a=norm_mix_a, w_in_a=w_in_a, b_igate=b_igate, b_fgate=b_fgate, headnorm_a=headnorm_a,
               w_out_a=w_out_a, norm_kv=norm_kv, w_kv=w_kv, k_norm=k_norm, norm_mix_b=norm_mix_b, w_q_b=w_q_b,
               q_norm_b=q_norm_b, sb_bias=sb_bias, w_out_b=w_out_b, norm_ffn=norm_ffn, w_gate=w_gate, w_up=w_up,
               conv_w=conv_w, conv_b=conv_b, w_down=w_down)
    bp, s, d = x_prompt.shape
    yp, kp, vp, cp, np_, mp, convp = _trunk(x_prompt.reshape(bp * s, d), prm, seq_len=s, tm=min(1024, s))
    db, ds, _ = x_sample.shape
    assert ds == 1
    ys, ks, vs, cs, ns, ms, convs = _trunk(
        x_sample.reshape(db, d), prm, seq_len=None, tm=db,
        mlstm_state=(state_mlstm_C[0], state_mlstm_n[0], state_mlstm_m[0]),
        conv_state=state_conv, paged=(cache_k, cache_v, page_table))
    return (yp.reshape(bp, s, d), ys.reshape(db, 1, d), kp, vp, cp, np_, mp, convp,
            ks, vs, cs, ns, ms, convs)
```

```python
import functools

import jax
import jax.numpy as jnp
from jax import lax
from jax.experimental import pallas as pl
from jax.experimental.pallas import tpu as pltpu

F32 = jnp.float32
BF16 = jnp.bfloat16
EPS = 1e-6
MLSTM_CHUNK = 64
MIB = 1024 * 1024
LANES = 128
SUBLANES = 8
PAGES_PER_STEP = 4
PAGED_BUFFERS = 4
FFN_SUB_ROWS = 512
FFN_TF = 512
MATMUL_SUB_ROWS = 256
PROJ_ROWS = 2048


def _params(semantics, vmem_mib):
    return pltpu.CompilerParams(dimension_semantics=semantics, vmem_limit_bytes=vmem_mib * MIB)


def _rms(x):
    return x * lax.rsqrt(jnp.mean(x * x, axis=-1, keepdims=True) + EPS)


def _log_sigmoid(x):
    return jnp.minimum(x, 0.0) - jnp.log1p(jnp.exp(-jnp.abs(x)))


def _norm_matmul_kernel(*refs, head_dim, n_normed_tiles, has_small, sub_rows, w_transposed):
    x_ref, g_ref, w_ref, hg_ref = refs[:4]
    rest = refs[4:]
    if has_small:
        ws_ref, o_ref, os_ref, h_scr = rest
    else:
        o_ref, h_scr = rest
    j = pl.program_id(1)
    dims = (((1,), (1 if w_transposed else 0,)), ((), ()))

    @pl.when(j == 0)
    def _():
        h = _rms(x_ref[...]) * g_ref[...]
        h_scr[...] = h.astype(BF16)
        if has_small:
            os_ref[...] = lax.dot_general(h, ws_ref[...], dims, preferred_element_type=F32)

    tm, tn = o_ref.shape
    sub = min(tm, sub_rows)
    for s in range(tm // sub):
        rows = slice(s * sub, (s + 1) * sub)
        y = lax.dot_general(h_scr[rows, :], w_ref[...].astype(BF16), dims, preferred_element_type=F32)
        if n_normed_tiles > 0:
            normed = jnp.concatenate(
                [_rms(y[:, c * head_dim:(c + 1) * head_dim]) * hg_ref[...] for c in range(tn // head_dim)],
                axis=1)
            y = jnp.where(j < n_normed_tiles, normed, y)
        o_ref[rows, :] = y


def norm_matmul(x, g, w, *, n_out, tm, tn, head_gain=None, n_normed_cols=0, w_small=None, w_transposed=False):
    m, k = x.shape
    head_dim = LANES if head_gain is None else head_gain.shape[-1]
    hg = jnp.ones((1, head_dim), F32) if head_gain is None else head_gain.reshape(1, head_dim)
    assert m % tm == 0 and tn % head_dim == 0 and n_out % tn == 0 and n_normed_cols % tn == 0
    has_small = w_small is not None
    in_specs = [
        pl.BlockSpec((tm, k), lambda i, j: (i, 0), pipeline_mode=pl.Buffered(1)),
        pl.BlockSpec((1, k), lambda i, j: (0, 0)),
        pl.BlockSpec((tn, k), lambda i, j: (j, 0)) if w_transposed else pl.BlockSpec((k, tn), lambda i, j: (0, j)),
        pl.BlockSpec((1, head_dim), lambda i, j: (0, 0)),
    ]
    args = [x, g.reshape(1, k), w, hg]
    out_shape = [jax.ShapeDtypeStruct((m, n_out), F32)]
    out_specs = [pl.BlockSpec((tm, tn), lambda i, j: (i, j))]
    if has_small:
        ns = w_small.shape[0 if w_transposed else 1]
        in_specs.append(pl.BlockSpec(w_small.shape, lambda i, j: (0, 0)))
        args.append(w_small)
        out_shape.append(jax.ShapeDtypeStruct((m, ns), F32))
        out_specs.append(pl.BlockSpec((tm, ns), lambda i, j: (i, 0)))
    outs = pl.pallas_call(
        functools.partial(_norm_matmul_kernel, head_dim=head_dim, n_normed_tiles=n_normed_cols // tn,
                          has_small=has_small, sub_rows=MATMUL_SUB_ROWS if n_normed_cols else tm,
                          w_transposed=w_transposed),
        grid=(m // tm, n_out // tn),
        in_specs=in_specs,
        out_specs=out_specs,
        out_shape=out_shape,
        scratch_shapes=[pltpu.VMEM((tm, k), BF16)],
        compiler_params=_params(("arbitrary", "arbitrary"), 56),
        name="norm_matmul",
    )(*args)
    return outs if has_small else outs[0]


def _split_heads_kernel(*refs, heads, hd):
    n = len(refs) // 2
    for x_ref, o_ref in zip(refs[:n], refs[n:]):
        tm = x_ref.shape[0]
        for c in range(heads):
            o_ref[pl.ds(c, tm, stride=heads), :] = x_ref[:, c * hd:(c + 1) * hd]


def split_heads(xs, col_blocks, *, heads, hd, tm):
    m = xs[0].shape[0]
    width = heads * hd
    assert m % tm == 0
    outs = pl.pallas_call(
        functools.partial(_split_heads_kernel, heads=heads, hd=hd),
        grid=(m // tm,),
        in_specs=[pl.BlockSpec((tm, width), functools.partial(lambda c, i: (i, c), c)) for c in col_blocks],
        out_specs=[pl.BlockSpec((tm * heads, hd), lambda i: (i, 0))] * len(xs),
        out_shape=[jax.ShapeDtypeStruct((m * heads, hd), F32)] * len(xs),
        compiler_params=_params(("arbitrary",), 32),
        name="split_heads",
    )(*xs)
    return [o.reshape(m, heads, hd) for o in outs]


def _matmul_res_kernel(a_ref, w_ref, r_ref, o_ref):
    o_ref[...] = r_ref[...] + jnp.dot(a_ref[...], w_ref[...], preferred_element_type=F32)


def matmul_residual(a, w, res, *, tm, tn):
    m, k = a.shape
    n = w.shape[1]
    assert m % tm == 0 and n % tn == 0
    return pl.pallas_call(
        _matmul_res_kernel,
        grid=(m // tm, n // tn),
        in_specs=[
            pl.BlockSpec((tm, k), lambda i, j: (i, 0), pipeline_mode=pl.Buffered(1)),
            pl.BlockSpec((k, tn), lambda i, j: (0, j)),
            pl.BlockSpec((tm, tn), lambda i, j: (i, j)),
        ],
        out_specs=pl.BlockSpec((tm, tn), lambda i, j: (i, j)),
        out_shape=jax.ShapeDtypeStruct((m, n), F32),
        compiler_params=_params(("arbitrary", "arbitrary"), 48),
        name="matmul_residual",
    )(a, w, res)


def _ffn_kernel(*refs, seq_tiles, per_row_state, sub_rows):
    if per_row_state:
        (x_ref, g_ref, wg_ref, wu_ref, wd_ref, cw_ref, cb_ref, p2_ref, p1_ref,
         o_ref, tail_ref, h_scr) = refs
    else:
        (x_ref, g_ref, wg_ref, wu_ref, wd_ref, cw_ref, cb_ref,
         o_ref, tail_ref, h_scr, carry_scr) = refs
    i = pl.program_id(0)
    f = pl.program_id(1)

    @pl.when(f == 0)
    def _():
        x = x_ref[...]
        h_scr[...] = (_rms(x) * g_ref[...]).astype(BF16)
        o_ref[...] = x

    tm = x_ref.shape[0]
    sub = min(tm, sub_rows)
    if not per_row_state:
        @pl.when(i % seq_tiles == 0)
        def _():
            carry_scr[f] = jnp.zeros(carry_scr.shape[1:], F32)

        prev = carry_scr[f, 0:2, :]
        row = lax.broadcasted_iota(jnp.int32, (sub, 1), 0)
    for s in range(tm // sub):
        rows = slice(s * sub, (s + 1) * sub)
        h = h_scr[rows, :]
        u = jnp.dot(h, wg_ref[...].astype(BF16), preferred_element_type=F32)
        up = jnp.dot(h, wu_ref[...].astype(BF16), preferred_element_type=F32)
        if per_row_state:
            u1 = p1_ref[rows, :]
            u2 = p2_ref[rows, :]
            tail_ref[rows, :] = u
        else:
            u1 = jnp.where(row == 0, prev[1:2], pltpu.roll(u, 1, axis=0))
            u2 = jnp.where(row == 0, prev[0:1], jnp.where(row == 1, prev[1:2], pltpu.roll(u, 2, axis=0)))
            prev = u[sub - 2:sub]
        conv = cb_ref[...] + cw_ref[0:1] * u2 + cw_ref[1:2] * u1 + cw_ref[2:3] * u
        act = conv * jax.nn.sigmoid(conv) * up
        o_ref[rows, :] += jnp.dot(act.astype(BF16), wd_ref[...].astype(BF16), preferred_element_type=F32)
    if not per_row_state:
        tail_ref[0] = prev
        carry_scr[f, 0:2, :] = prev


def conv_ffn(x, g, w_gate, w_up, w_down, conv_w, conv_b, *, layer, tm, tf, seq_len=None, state=None):
    m, d = x.shape
    nf = w_gate.shape[2]
    assert m % tm == 0 and nf % tf == 0
    per_row_state = state is not None
    grid = (m // tm, nf // tf)
    row_spec = pl.BlockSpec((tm, d), lambda i, f: (i, 0), pipeline_mode=pl.Buffered(1))
    in_specs = [
        row_spec,
        pl.BlockSpec((1, d), lambda i, f: (0, 0)),
        pl.BlockSpec((None, d, tf), lambda i, f: (layer, 0, f)),
        pl.BlockSpec((None, d, tf), lambda i, f: (layer, 0, f)),
        pl.BlockSpec((None, tf, d), lambda i, f: (layer, f, 0)),
        pl.BlockSpec((3, tf), lambda i, f: (0, f)),
        pl.BlockSpec((1, tf), lambda i, f: (0, f)),
    ]
    args = [x, g.reshape(1, d), w_gate, w_up, w_down, conv_w, conv_b.reshape(1, nf)]
    scratch = [pltpu.VMEM((tm, d), BF16)]
    if per_row_state:
        in_specs += [pl.BlockSpec((tm, tf), lambda i, f: (i, f))] * 2
        args += [state[0], state[1]]
        tail_shape = jax.ShapeDtypeStruct((m, nf), F32)
        tail_spec = pl.BlockSpec((tm, tf), lambda i, f: (i, f))
        seq_tiles = 1
    else:
        assert seq_len % tm == 0
        seq_tiles = seq_len // tm
        tail_shape = jax.ShapeDtypeStruct((m // tm, 2, nf), F32)
        tail_spec = pl.BlockSpec((1, 2, tf), lambda i, f: (i, 0, f))
        scratch.append(pltpu.VMEM((nf // tf, SUBLANES, tf), F32))
    return pl.pallas_call(
        functools.partial(_ffn_kernel, seq_tiles=seq_tiles, per_row_state=per_row_state,
                          sub_rows=FFN_SUB_ROWS),
        grid=grid,
        in_specs=in_specs,
        out_specs=[row_spec, tail_spec],
        out_shape=[jax.ShapeDtypeStruct((m, d), F32), tail_shape],
        scratch_shapes=scratch,
        compiler_params=_params(("arbitrary", "arbitrary"), 56),
        name="conv_ffn_rows" if per_row_state else "conv_ffn_seq",
    )(*args)


def _mlstm_seq_kernel(q_ref, k_ref, v_ref, o_ref, gt_ref, bias_ref, hn_ref,
                      hg_ref, c_ref, n_ref, m_ref, *, heads, dk, dv, chunk):
    c = pl.program_id(1)

    @pl.when(c == 0)
    def _():
        c_ref[...] = jnp.zeros_like(c_ref)
        n_ref[...] = jnp.zeros_like(n_ref)
        m_ref[...] = jnp.zeros_like(m_ref)

    gates = gt_ref[0] + bias_ref[...]
    ig_all = gates[:, :heads]
    lf_all = _log_sigmoid(gates[:, heads:])
    ti = lax.broadcasted_iota(jnp.int32, (chunk, chunk), 0)
    si = lax.broadcasted_iota(jnp.int32, (chunk, chunk), 1)
    eye = ti == si
    lower = si <= ti
    hs = range(heads)
    qsl = [slice(h * dk, (h + 1) * dk) for h in hs]
    vsl = [slice(h * dv, (h + 1) * dv) for h in hs]
    q = [q_ref[0, :, qsl[h]] for h in hs]
    k = [k_ref[0, :, qsl[h]] * (dk ** -0.5) for h in hs]
    v = [v_ref[0, :, vsl[h]] for h in hs]
    ig_c = [ig_all[:, h:h + 1] for h in hs]
    lf_c = [lf_all[:, h:h + 1] for h in hs]
    m_prev = [m_ref[0, h:h + 1, 0:1] for h in hs]
    n_h = [n_ref[0, h:h + 1, :] for h in hs]
    lf_r = [jnp.sum(jnp.where(eye, lf_c[h], 0.0), axis=0, keepdims=True) for h in hs]
    ig_r = [jnp.sum(jnp.where(eye, ig_c[h], 0.0), axis=0, keepdims=True) for h in hs]
    b_r = [jnp.sum(jnp.where(ti <= si, lf_c[h], 0.0), axis=0, keepdims=True) for h in hs]
    b_c = [jnp.sum(jnp.where(lower, lf_r[h], 0.0), axis=1, keepdims=True) for h in hs]
    d = [jnp.where(lower, b_c[h] - b_r[h] + ig_r[h], -jnp.inf) for h in hs]
    d_max = [jnp.max(d[h], axis=1, keepdims=True) for h in hs]
    inter = [b_c[h] + m_prev[h] for h in hs]
    m_row = [jnp.maximum(inter[h], d_max[h]) for h in hs]
    qk = [lax.dot_general(q[h], k[h], (((1,), (1,)), ((), ())), preferred_element_type=F32) for h in hs]
    w = [jnp.exp(d[h] - m_row[h]) * qk[h] for h in hs]
    w_inter = [jnp.exp(inter[h] - m_row[h]) for h in hs]
    w_sum = [jnp.sum(w[h], axis=1, keepdims=True) for h in hs]
    qn = [jnp.sum(q[h] * n_h[h], axis=1, keepdims=True) for h in hs]
    num = [jnp.dot(w[h], v[h], preferred_element_type=F32)
           + w_inter[h] * jnp.dot(q[h], c_ref[0, h], preferred_element_type=F32) for h in hs]
    den = [jnp.maximum(jnp.abs(w_sum[h] + w_inter[h] * qn[h]), jnp.exp(-m_row[h])) for h in hs]
    hc = [num[h] / den[h] for h in hs]
    hc_ms = [jnp.mean(hc[h] * hc[h], axis=-1, keepdims=True) for h in hs]
    for h in hs:
        hn = hc[h] * lax.rsqrt(hc_ms[h] + EPS) * hn_ref[:, vsl[h]]
        hg_ref[0, :, vsl[h]] = hn * jax.nn.sigmoid(o_ref[0, :, vsl[h]])
    g = [b_c[h][chunk - 1:chunk, :] for h in hs]
    log_ws = [g[h] - b_c[h] + ig_c[h] for h in hs]
    m_new = [jnp.maximum(g[h] + m_prev[h], jnp.max(log_ws[h], axis=0, keepdims=True)) for h in hs]
    ws = [jnp.exp(log_ws[h] - m_new[h]) for h in hs]
    decay = [jnp.exp(g[h] + m_prev[h] - m_new[h]) for h in hs]
    for h in hs:
        c_ref[0, h] = decay[h] * c_ref[0, h] + lax.dot_general(
            k[h], ws[h] * v[h], (((0,), (0,)), ((), ())), preferred_element_type=F32)
        n_ref[0, h:h + 1, :] = decay[h] * n_h[h] + jnp.sum(ws[h] * k[h], axis=0, keepdims=True)
        m_ref[0, h:h + 1, :] = jnp.broadcast_to(m_new[h], (1, m_ref.shape[2]))


def mlstm_seq(p, gates, bias, headnorm, *, heads, dk, dv):
    bsz, s, _ = p.shape
    chunk = MLSTM_CHUNK if s % MLSTM_CHUNK == 0 else s
    nc = s // chunk
    wq = heads * dk
    wv = heads * dv
    assert wv == 2 * wq
    return pl.pallas_call(
        functools.partial(_mlstm_seq_kernel, heads=heads, dk=dk, dv=dv, chunk=chunk),
        grid=(bsz, nc),
        in_specs=[
            pl.BlockSpec((1, chunk, wq), lambda b, c: (b, c, 0)),
            pl.BlockSpec((1, chunk, wq), lambda b, c: (b, c, 1)),
            pl.BlockSpec((1, chunk, wv), lambda b, c: (b, c, 1)),
            pl.BlockSpec((1, chunk, wv), lambda b, c: (b, c, 2)),
            pl.BlockSpec((1, chunk, 2 * heads), lambda b, c: (b, c, 0)),
            pl.BlockSpec((1, 2 * heads), lambda b, c: (0, 0)),
            pl.BlockSpec((1, wv), lambda b, c: (0, 0)),
        ],
        out_specs=[
            pl.BlockSpec((1, chunk, wv), lambda b, c: (b, c, 0)),
            pl.BlockSpec((1, heads, dk, dv), lambda b, c: (b, 0, 0, 0)),
            pl.BlockSpec((1, heads, dk), lambda b, c: (b, 0, 0)),
            pl.BlockSpec((1, heads, dk), lambda b, c: (b, 0, 0)),
        ],
        out_shape=[
            jax.ShapeDtypeStruct((bsz, s, wv), F32),
            jax.ShapeDtypeStruct((bsz, heads, dk, dv), F32),
            jax.ShapeDtypeStruct((bsz, heads, dk), F32),
            jax.ShapeDtypeStruct((bsz, heads, dk), F32),
        ],
        compiler_params=_params(("arbitrary", "arbitrary"), 32),
        name="mlstm_seq",
    )(p, p, p, p, gates, bias, headnorm)


def _mlstm_step_kernel(q_ref, k_ref, v_ref, o_ref, gt_ref, bias_ref, hn_ref, c0_ref, n0_ref, m0_ref,
                       hg_ref, c_ref, n_ref, m_ref, *, heads, dk, dv):
    gates = gt_ref[0] + bias_ref[...]
    ig_all = gates[:, :heads]
    lf_all = _log_sigmoid(gates[:, heads:])
    ri = lax.broadcasted_iota(jnp.int32, (dk, dk), 0)
    ci = lax.broadcasted_iota(jnp.int32, (dk, dk), 1)
    eye = ri == ci
    hs = range(heads)
    qsl = [slice(h * dk, (h + 1) * dk) for h in hs]
    vsl = [slice(h * dv, (h + 1) * dv) for h in hs]
    q = [q_ref[0, :, qsl[h]] for h in hs]
    k = [k_ref[0, :, qsl[h]] * (dk ** -0.5) for h in hs]
    v = [v_ref[0, :, vsl[h]] for h in hs]
    ig = [ig_all[:, h:h + 1] for h in hs]
    lf = [lf_all[:, h:h + 1] for h in hs]
    m_prev = [m0_ref[0, h:h + 1, 0:1] for h in hs]
    n_h = [n0_ref[0, h:h + 1, :] for h in hs]
    inter = [lf[h] + m_prev[h] for h in hs]
    m_row = [jnp.maximum(inter[h], ig[h]) for h in hs]
    qk = [jnp.sum(q[h] * k[h], axis=1, keepdims=True) for h in hs]
    qn = [jnp.sum(q[h] * n_h[h], axis=1, keepdims=True) for h in hs]
    k_col = [jnp.sum(jnp.where(eye, k[h], 0.0), axis=1, keepdims=True) for h in hs]
    w = [jnp.exp(ig[h] - m_row[h]) * qk[h] for h in hs]
    w_inter = [jnp.exp(inter[h] - m_row[h]) for h in hs]
    qc = [jnp.dot(jnp.broadcast_to(q[h], (SUBLANES, dk)), c0_ref[0, h], preferred_element_type=F32)[0:1]
          for h in hs]
    num = [w[h] * v[h] + w_inter[h] * qc[h] for h in hs]
    den = [jnp.maximum(jnp.abs(w[h] + w_inter[h] * qn[h]), jnp.exp(-m_row[h])) for h in hs]
    hc = [num[h] / den[h] for h in hs]
    hc_ms = [jnp.mean(hc[h] * hc[h], axis=-1, keepdims=True) for h in hs]
    m_new = [jnp.maximum(lf[h] + m_prev[h], ig[h]) for h in hs]
    ws = [jnp.exp(ig[h] - m_new[h]) for h in hs]
    decay = [jnp.exp(lf[h] + m_prev[h] - m_new[h]) for h in hs]
    for h in hs:
        hn = hc[h] * lax.rsqrt(hc_ms[h] + EPS) * hn_ref[:, vsl[h]]
        hg_ref[0, :, vsl[h]] = hn * jax.nn.sigmoid(o_ref[0, :, vsl[h]])
        c_ref[0, h] = decay[h] * c0_ref[0, h] + k_col[h] * (ws[h] * v[h])
        n_ref[0, h:h + 1, :] = decay[h] * n_h[h] + ws[h] * k[h]
        m_ref[0, h:h + 1, :] = jnp.broadcast_to(m_new[h], (1, m_ref.shape[2]))


def mlstm_step(p, gates, bias, headnorm, c0, n0, m0b, *, heads, dk, dv):
    bsz = p.shape[0]
    wq = heads * dk
    wv = heads * dv
    state_specs = [
        pl.BlockSpec((1, heads, dk, dv), lambda b: (b, 0, 0, 0)),
        pl.BlockSpec((1, heads, dk), lambda b: (b, 0, 0)),
        pl.BlockSpec((1, heads, dk), lambda b: (b, 0, 0)),
    ]
    return pl.pallas_call(
        functools.partial(_mlstm_step_kernel, heads=heads, dk=dk, dv=dv),
        grid=(bsz,),
        in_specs=[
            pl.BlockSpec((1, 1, wq), lambda b: (b, 0, 0)),
            pl.BlockSpec((1, 1, wq), lambda b: (b, 0, 1)),
            pl.BlockSpec((1, 1, wv), lambda b: (b, 0, 1)),
            pl.BlockSpec((1, 1, wv), lambda b: (b, 0, 2)),
            pl.BlockSpec((1, 1, 2 * heads), lambda b: (b, 0, 0)),
            pl.BlockSpec((1, 2 * heads), lambda b: (0, 0)),
            pl.BlockSpec((1, wv), lambda b: (0, 0)),
        ] + state_specs,
        out_specs=[pl.BlockSpec((1, 1, wv), lambda b: (b, 0, 0))] + state_specs,
        out_shape=[
            jax.ShapeDtypeStruct((bsz, 1, wv), F32),
            jax.ShapeDtypeStruct(c0.shape, F32),
            jax.ShapeDtypeStruct(n0.shape, F32),
            jax.ShapeDtypeStruct(n0.shape, F32),
        ],
        compiler_params=_params(("arbitrary",), 32),
        name="mlstm_step",
    )(p, p, p, p, gates, bias, headnorm, c0, n0, m0b)


def _softplus_pair(z):
    l = jnp.log(1.0 + jnp.exp(-jnp.abs(z)))
    return jnp.maximum(z, 0.0) + l, jnp.minimum(z, 0.0) - l


def _softplus(z):
    return jnp.maximum(z, 0.0) + jnp.log(1.0 + jnp.exp(-jnp.abs(z)))


def _sb_seq_kernel(bias_ref, q_ref, k_ref, v_ref, o_ref, *, tq, scale):
    h = pl.program_id(1)
    s_len = q_ref.shape[1]
    nq = s_len // tq
    bias = bias_ref[h]
    ji = lax.broadcasted_iota(jnp.int32, (tq, tq), 0)
    si = lax.broadcasted_iota(jnp.int32, (tq, tq), 1)
    later = jnp.where(ji > si, 1.0, 0.0).astype(F32)
    causal = si < ji
    qs = q_ref[0] * scale
    run = None

    def diag_masked(x):
        top = jnp.where(causal, x[:tq], 0.0)
        return top if x.shape[0] == tq else jnp.concatenate([top, x[tq:]], axis=0)

    for kb in range(nq - 1, -1, -1):
        lo = kb * tq
        kk = k_ref[0, lo:lo + tq, :]
        vv = v_ref[0, lo:lo + tq, :]
        z = lax.dot_general(qs[lo:], kk, (((1,), (1,)), ((), ())), preferred_element_type=F32) + bias
        sp = _softplus(z)
        ls = z - sp
        sp = diag_masked(sp)
        fresh = jnp.zeros((tq, 1), F32)
        run = fresh if run is None else jnp.concatenate([fresh, run], axis=0)
        after = jnp.dot(sp, later, preferred_element_type=F32) + run
        a = jnp.exp(ls - after)
        a = diag_masked(a)
        contrib = jnp.dot(a, vv, preferred_element_type=F32)
        o_ref[0, lo:lo + tq, :] = contrib[:tq]
        if kb < nq - 1:
            o_ref[0, lo + tq:, :] += contrib[tq:]
        run = run + jnp.sum(sp, axis=1, keepdims=True)


def sb_attention_seq(q, k, v, bias, *, heads, tq, k_block0=0, v_block0=0):
    bsz, s, width = q.shape
    hd = width // heads
    assert s % tq == 0
    return pl.pallas_call(
        functools.partial(_sb_seq_kernel, tq=tq, scale=hd ** -0.5),
        grid=(bsz, heads),
        in_specs=[
            pl.BlockSpec(memory_space=pltpu.SMEM),
            pl.BlockSpec((1, s, hd), lambda b, h: (b, 0, h)),
            pl.BlockSpec((1, s, hd), lambda b, h: (b, 0, k_block0 + h)),
            pl.BlockSpec((1, s, hd), lambda b, h: (b, 0, v_block0 + h)),
        ],
        out_specs=pl.BlockSpec((1, s, hd), lambda b, h: (b, 0, h)),
        out_shape=jax.ShapeDtypeStruct((bsz, s, width), F32),
        compiler_params=_params(("arbitrary", "arbitrary"), 48),
        name="sb_attention_seq",
    )(bias, q, k, v)


def _sb_paged_kernel(pt_ref, bias_ref, ts_ref, q_ref, ck_hbm, cv_hbm, o_ref, kbuf, vbuf, sem, acc_scr, run_scr,
                     *, heads, page, scale, pages_per_step, n_pages, n_buf):
    p = pl.program_id(1)
    n_inner = pl.num_programs(1)
    step = pl.program_id(0) * n_inner + p
    n_steps = pl.num_programs(0) * n_inner

    def page_copies(s):
        sb, sp_ = s // n_inner, s % n_inner
        slot = s % n_buf
        out = []
        for j in range(pages_per_step):
            pg = pt_ref[sb, n_pages - 1 - (sp_ * pages_per_step + j)]
            out.append(pltpu.make_async_copy(ck_hbm.at[pg], kbuf.at[slot, j], sem.at[slot, 0, j]))
            out.append(pltpu.make_async_copy(cv_hbm.at[pg], vbuf.at[slot, j], sem.at[slot, 1, j]))
        return out

    @pl.when(step == 0)
    def _():
        for s in range(n_buf - 1):
            for c in page_copies(s):
                c.start()

    @pl.when(step + (n_buf - 1) < n_steps)
    def _():
        for c in page_copies(step + (n_buf - 1)):
            c.start()

    for c in page_copies(step):
        c.wait()
    slot = step % n_buf
    kv_refs = []
    for j in range(pages_per_step):
        kv_refs += [kbuf.at[slot, j], vbuf.at[slot, j]]

    @pl.when(p == 0)
    def _():
        acc_scr[...] = jnp.zeros_like(acc_scr)
        run_scr[...] = jnp.zeros_like(run_scr)

    q = q_ref[0]
    tile = 2 * LANES
    n_tiles = page * heads // tile
    hrow = lax.broadcasted_iota(jnp.int32, (heads, tile), 0)
    lane = lax.broadcasted_iota(jnp.int32, (heads, tile), 1)
    own = (lane % heads) == hrow
    trow = lax.broadcasted_iota(jnp.int32, (n_tiles, 1), 0)

    def shift_up(y, k):
        return jnp.where(trow < n_tiles - k, pltpu.roll(y, n_tiles - k, axis=0), 0.0)

    acc = acc_scr[...]
    run = run_scr[...]
    for j in range(pages_per_step):
        k_ref, v_ref = kv_refs[2 * j], kv_refs[2 * j + 1]
        r = lax.dot_general(q, k_ref[...], (((1,), (1,)), ((), ())), preferred_element_type=F32)
        z = jnp.zeros((n_tiles, tile), F32)
        for t in range(n_tiles):
            zt = jnp.sum(jnp.where(own, r[:, t * tile:(t + 1) * tile], 0.0), axis=0, keepdims=True)
            z = jnp.where(trow == t, zt, z)
        z = z * scale + bias_ref[...]
        sp, ls = _softplus_pair(z)
        st = jnp.dot(sp, ts_ref[...], preferred_element_type=F32)
        loc = st[:, :tile]
        tot = st[:, tile:]
        e = shift_up(tot, 1)
        k = 1
        while k < n_tiles:
            e = e + shift_up(e, k)
            k *= 2
        a = jnp.exp(ls - (loc + e + run))
        run = run + e[0:1] + tot[0:1]
        a_big = jnp.concatenate([jnp.where(own, a[t:t + 1, :], 0.0) for t in range(n_tiles)], axis=1)
        acc = acc + jnp.dot(a_big, v_ref[...], preferred_element_type=F32)
    acc_scr[...] = acc
    run_scr[...] = run

    @pl.when(p == pl.num_programs(1) - 1)
    def _():
        o_ref[0] = acc


def sb_attention_paged(q, cache_k, cache_v, page_table, bias):
    bsz, heads, hd = q.shape
    n_pool, page = cache_k.shape[:2]
    n_pages = page_table.shape[1]
    pps = PAGES_PER_STEP if n_pages % PAGES_PER_STEP == 0 else 1
    tile = 2 * LANES
    assert tile % heads == 0 and (page * heads) % tile == 0
    ck = cache_k.reshape(n_pool, page * heads, hd)
    cv = cache_v.reshape(n_pool, page * heads, hd)
    idx = jnp.arange(tile)
    same = (idx[:, None] % heads) == (idx[None, :] % heads)
    later = same & ((idx[:, None] // heads) > (idx[None, :] // heads))
    ts = jnp.concatenate([later, same], axis=1).astype(F32)
    bias_lanes = jnp.tile(bias, tile // heads).reshape(1, tile)

    n_buf = min(PAGED_BUFFERS, bsz * (n_pages // pps))
    page_buf = pltpu.VMEM((n_buf, pps, page * heads, hd), F32)
    return pl.pallas_call(
        functools.partial(_sb_paged_kernel, heads=heads, page=page, scale=hd ** -0.5, pages_per_step=pps,
                          n_pages=n_pages, n_buf=n_buf),
        grid_spec=pltpu.PrefetchScalarGridSpec(
            num_scalar_prefetch=1,
            grid=(bsz, n_pages // pps),
            in_specs=[
                pl.BlockSpec((1, tile), lambda b, p, pt: (0, 0)),
                pl.BlockSpec((tile, 2 * tile), lambda b, p, pt: (0, 0)),
                pl.BlockSpec((1, heads, hd), lambda b, p, pt: (b, 0, 0)),
                pl.BlockSpec(memory_space=pl.ANY),
                pl.BlockSpec(memory_space=pl.ANY),
            ],
            out_specs=pl.BlockSpec((1, heads, hd), lambda b, p, pt: (b, 0, 0)),
            scratch_shapes=[page_buf, page_buf, pltpu.SemaphoreType.DMA((n_buf, 2, pps)),
                            pltpu.VMEM((heads, hd), F32), pltpu.VMEM((1, tile), F32)],
        ),
        out_shape=jax.ShapeDtypeStruct((bsz, heads, hd), F32),
        compiler_params=_params(("arbitrary", "arbitrary"), 48),
        name="sb_attention_paged",
    )(page_table, bias_lanes, ts, q, ck, cv)


def _trunk(x, prm, *, seq_len, tm, mlstm_state=None, conv_state=None, paged=None):
    m, d = x.shape
    heads_a, dv = prm["headnorm_a"].shape[1:]
    dk = (prm["w_in_a"].shape[2] - 2 * heads_a * dv - 2 * heads_a) // (2 * heads_a)
    heads_b = prm["sb_bias"].shape[1]
    hd = prm["k_norm"].shape[0]
    wide = 2 * heads_a * dk + 2 * heads_a * dv
    is_seq = seq_len is not None
    bsz = m // seq_len if is_seq else m
    tp = min(m, PROJ_ROWS)

    w_in_t = jnp.swapaxes(prm["w_in_a"][0], 0, 1)
    p, gates = norm_matmul(x, prm["norm_mix_a"][0], w_in_t, n_out=wide, tm=tp, tn=512,
                           w_small=w_in_t[wide:], w_transposed=True)
    gate_bias = jnp.concatenate([prm["b_igate"][0], prm["b_fgate"][0]]).reshape(1, 2 * heads_a)
    headnorm = prm["headnorm_a"][0].reshape(1, heads_a * dv)
    if is_seq:
        hg, c_new, n_new, m_new = mlstm_seq(p.reshape(bsz, seq_len, wide), gates.reshape(bsz, seq_len, -1),
                                            gate_bias, headnorm, heads=heads_a, dk=dk, dv=dv)
    else:
        c0, n0, m0 = mlstm_state
        m0b = jnp.broadcast_to(m0[..., None], n0.shape)
        hg, c_new, n_new, m_new = mlstm_step(p.reshape(m, 1, wide), gates.reshape(m, 1, -1), gate_bias,
                                             headnorm, c0, n0, m0b, heads=heads_a, dk=dk, dv=dv)
    m_new = m_new[:, :, 0]
    x = matmul_residual(hg.reshape(m, heads_a * dv), prm["w_out_a"][0], x, tm=tp, tn=512)

    def ffn(x, layer):
        a = (x, prm["norm_ffn"][layer], prm["w_gate"], prm["w_up"], prm["w_down"],
             prm["conv_w"][layer], prm["conv_b"][layer])
        if is_seq:
            y, tails = conv_ffn(*a, layer=layer, tm=tm, tf=FFN_TF, seq_len=seq_len)
            st = seq_len // tm
            return y, tails[st - 1::st]
        buf = conv_state[layer]
        y, u = conv_ffn(*a, layer=layer, tm=tm, tf=FFN_TF, state=(buf[:, 0], buf[:, 1]))
        return y, jnp.stack([buf[:, 1], u], axis=1)

    x, conv0 = ffn(x, 0)

    width_b = heads_b * hd
    kv = norm_matmul(x, prm["norm_kv"], prm["w_kv"], n_out=2 * width_b, tm=tp, tn=512,
                     head_gain=prm["k_norm"], n_normed_cols=width_b)

    q = norm_matmul(x, prm["norm_mix_b"][0], prm["w_q_b"][0], n_out=width_b, tm=tp, tn=512,
                    head_gain=prm["q_norm_b"][0], n_normed_cols=width_b)
    if is_seq:
        kv3 = kv.reshape(bsz, seq_len, 2 * width_b)
        o = sb_attention_seq(q.reshape(bsz, seq_len, -1), kv3, kv3, prm["sb_bias"][0], heads=heads_b, tq=256,
                             k_block0=0, v_block0=heads_b)
        k_out, v_out = split_heads([kv, kv], [0, 1], heads=heads_b, hd=hd, tm=min(tm, 512))
    else:
        cache_k, cache_v, page_table = paged
        o = sb_attention_paged(q.reshape(m, heads_b, hd), cache_k, cache_v, page_table, prm["sb_bias"][0])
        k_out, v_out = kv[:, :width_b], kv[:, width_b:]
    x = matmul_residual(o.reshape(m, width_b), prm["w_out_b"][0], x, tm=tp, tn=512)
    x, conv1 = ffn(x, 1)
    return (x, k_out.reshape(bsz, -1, heads_b, hd), v_out.reshape(bsz, -1, heads_b, hd),
            c_new[None], n_new[None], m_new[None], jnp.stack([conv0, conv1]))


def kernel(x_prompt, x_sample, cache_k, cache_v, page_table, state_mlstm_C, state_mlstm_n, state_mlstm_m, state_conv, norm_mix_a, w_in_a, b_igate, b_fgate, headnorm_a, w_out_a, norm_kv, w_kv, k_norm, norm_mix_b, w_q_b, q_norm_b, sb_bias, w_out_b, norm_ffn, w_gate, w_up, conv_w, conv_b, w_down):
    prm = dict(norm_mix_a=norm_mix_a, w_in_a=w_in_a, b_igate=b_igate, b_fgate=b_fgate, headnorm_a=headnorm_a,
               w_out_a=w_out_a, norm_kv=norm_kv, w_kv=w_kv, k_norm=k_norm, norm_mix_b=norm_mix_b, w_q_b=w_q_b,
               q_norm_b=q_norm_b, sb_bias=sb_bias, w_out_b=w_out_b, norm_ffn=norm_ffn, w_gate=w_gate, w_up=w_up,
               conv_w=conv_w, conv_b=conv_b, w_down=w_down)
    bp, s, d = x_prompt.shape
    yp, kp, vp, cp, np_, mp, convp = _trunk(x_prompt.reshape(bp * s, d), prm, seq_len=s, tm=min(1024, s))
    db, ds, _ = x_sample.shape
    assert ds == 1
    ys, ks, vs, cs, ns, ms, convs = _trunk(
        x_sample.reshape(db, d), prm, seq_len=None, tm=db,
        mlstm_state=(state_mlstm_C[0], state_mlstm_n[0], state_mlstm_m[0]),
        conv_state=state_conv, paged=(cache_k, cache_v, page_table))
    return (yp.reshape(bp, s, d), ys.reshape(db, 1, d), kp, vp, cp, np_, mp, convp,
            ks, vs, cs, ns, ms, convs)
```
